```python
import math
import jax
import jax.numpy as jnp
from jax import lax
import numpy as np

D_MODEL = 2048
BATCH = 4
SEQ = 8192
DEPTH = 2

N_A_LAYERS = DEPTH // 2
N_B_LAYERS = DEPTH - N_A_LAYERS
RWKV_HEAD = 64
RWKV_HEADS = D_MODEL // RWKV_HEAD
DECAY_LORA = 96
AAA_LORA = 96
GATE_LORA = 256
N_MIX = 6
GN_EPS = 64e-5
DIFF_HEAD = 64
DIFF_HEADS = D_MODEL // (2 * DIFF_HEAD)
Q_BLOCK = 128
N_GROUPS = 4
EXPERTS_PER_GROUP = 8
N_EXPERTS = N_GROUPS * EXPERTS_PER_GROUP
TOP_K = 2
EXPERT_HIDDEN = D_MODEL // 2
MOE_BLOCK = 128
RMS_EPS = 1e-6

kernel_name = 'yoco_rwkv7_diffattn_hmoe'


def rms_norm(x, g):
    xf = x.astype(jnp.float32)
    y = xf * lax.rsqrt(jnp.mean(xf * xf, axis=-1, keepdims=True) + RMS_EPS)
    return (y * g.astype(jnp.float32)).astype(x.dtype)


def wkv7_scan(r, w, k, v, a, b):
    bsz, _, nh, hd = r.shape

    def step(state, inp):
        r_t, w_t, k_t, v_t, a_t, b_t = inp
        sa = jnp.einsum('bhvk,bhk->bhv', state, a_t)
        state = (state * w_t[:, :, None, :] + sa[..., None] * b_t[:, :, None, :]
                 + v_t[..., None] * k_t[:, :, None, :])
        return state, jnp.einsum('bhvk,bhk->bhv', state, r_t)

    xs = tuple(jnp.swapaxes(t, 0, 1) for t in (r, w, k, v, a, b))
    s0 = jnp.zeros((bsz, nh, hd, hd), jnp.float32)
    _, y = lax.scan(step, s0, xs)
    return jnp.swapaxes(y, 0, 1)


def rwkv7_time_mix(xn, mix, w_r, w_k, w_v, w0, w1, w2, a0, a1, a2, g1, g2,
                   k_k, k_a, r_k, gn_g, gn_b, w_o):
    bsz, seq, d = xn.shape

    def heads(t):
        return t.reshape(bsz, seq, RWKV_HEADS, RWKV_HEAD).astype(jnp.float32)

    xx = jnp.pad(xn[:, :-1], ((0, 0), (1, 0), (0, 0))) - xn
    xr, xw, xk, xv, xa, xg = [xn + xx * mix[i] for i in range(N_MIX)]
    r = xr @ w_r
    k = xk @ w_k
    v = xv @ w_v
    w_log = -jax.nn.softplus(-(w0 + jnp.tanh(xw @ w1) @ w2)) - 0.5
    decay = jnp.exp(-jnp.exp(w_log.astype(jnp.float32)))
    a = jax.nn.sigmoid(a0 + (xa @ a1) @ a2)
    g = jax.nn.sigmoid(xg @ g1) @ g2
    kk = heads(k * k_k)
    kk = kk * lax.rsqrt(jnp.maximum(jnp.sum(kk * kk, axis=-1, keepdims=True), 1e-24))
    k = k * (1 + (a - 1) * k_a)
    rh, kh, vh, ah = heads(r), heads(k), heads(v), heads(a)
    y = wkv7_scan(rh, heads(decay), kh, vh, -kk, kk * ah)
    yc = y - jnp.mean(y, axis=-1, keepdims=True)
    y = yc * lax.rsqrt(jnp.mean(yc * yc, axis=-1, keepdims=True) + GN_EPS)
    y = y.reshape(bsz, seq, d) * gn_g + gn_b
    bonus = jnp.sum(rh * kh * r_k, axis=-1, keepdims=True) * vh
    y = y + bonus.reshape(bsz, seq, d)
    return (y.astype(xn.dtype) * g) @ w_o


def shared_kv(h, kv_norm_g, w_kv, k_norm_g):
    bsz, seq, d = h.shape
    kv = rms_norm(h, kv_norm_g) @ w_kv
    k = rms_norm(kv[..., :d].reshape(bsz, seq, DIFF_HEADS, 2, DIFF_HEAD), k_norm_g)
    v = kv[..., d:].reshape(bsz, seq, DIFF_HEADS, 2 * DIFF_HEAD)
    return k, v


def diff_attn_block(q_blk, k_pre, v_pre, q_start, lam):
    n_q, n_k = q_blk.shape[1], k_pre.shape[1]
    s = jnp.einsum('bqhmd,bkhmd->bhmqk', q_blk, k_pre).astype(jnp.float32) * (DIFF_HEAD ** -0.5)
    causal = (q_start + jnp.arange(n_q))[:, None] >= jnp.arange(n_k)[None, :]
    s = jnp.where(causal, s, -jnp.inf)
    p = jax.nn.softmax(s, axis=-1)
    p = p[:, :, 0] - lam * p[:, :, 1]
    return jnp.einsum('bhqk,bkhe->bqhe', p.astype(v_pre.dtype), v_pre)


def diff_attention(hn, k, v, w_q, q_norm_g, lq1, lk1, lq2, lk2, subln_g, w_o, layer_idx):
    bsz, seq, d = hn.shape
    q = rms_norm((hn @ w_q).reshape(bsz, seq, DIFF_HEADS, 2, DIFF_HEAD), q_norm_g)
    lam_init = 0.8 - 0.6 * math.exp(-0.3 * layer_idx)
    f = lambda t: t.astype(jnp.float32)
    lam = jnp.exp(jnp.sum(f(lq1) * f(lk1))) - jnp.exp(jnp.sum(f(lq2) * f(lk2))) + lam_init
    outs = []
    for blk in range(seq // Q_BLOCK):
        q0 = blk * Q_BLOCK
        q1 = q0 + Q_BLOCK
        outs.append(diff_attn_block(q[:, q0:q1], k[:, :q1], v[:, :q1], q0, lam))
    o = jnp.concatenate(outs, axis=1)
    o = rms_norm(o, subln_g) * (1.0 - lam_init)
    return o.reshape(bsz, seq, d) @ w_o


def swiglu(xb, wg, wu, wd):
    return (jax.nn.silu(xb @ wg) * (xb @ wu)) @ wd


def routed_experts(t, eidx, gate, wg, wu, wd):
    n_tok, d = t.shape
    n_assign = n_tok * TOP_K
    n_pad = n_assign + N_EXPERTS * MOE_BLOCK
    flat_e = eidx.reshape(-1)
    flat_tok = jnp.arange(n_assign, dtype=jnp.int32) // TOP_K
    flat_w = gate.reshape(-1)
    order = jnp.argsort(flat_e)
    sorted_e = flat_e[order]
    counts = jnp.bincount(flat_e, length=N_EXPERTS)
    padded = (counts + MOE_BLOCK - 1) // MOE_BLOCK * MOE_BLOCK
    padded_end = jnp.cumsum(padded)
    padded_start = padded_end - padded
    start = jnp.cumsum(counts) - counts
    dest = padded_start[sorted_e] + jnp.arange(n_assign, dtype=jnp.int32) - start[sorted_e]
    buf_tok = jnp.zeros((n_pad,), jnp.int32).at[dest].set(flat_tok[order])
    buf_w = jnp.zeros((n_pad,), flat_w.dtype).at[dest].set(flat_w[order])
    n_blocks = n_pad // MOE_BLOCK
    block_start = jnp.arange(n_blocks, dtype=jnp.int32) * MOE_BLOCK
    block_e = jnp.minimum(jnp.searchsorted(padded_end, block_start, side='right'), N_EXPERTS - 1)
    xb = t[buf_tok].reshape(n_blocks, MOE_BLOCK, d)
    yb = lax.map(lambda args: swiglu(args[0], wg[args[1]], wu[args[1]], wd[args[1]]), (xb, block_e))
    return jnp.zeros_like(t).at[buf_tok].add(yb.reshape(n_pad, d) * buf_w[:, None])


def hier_moe(hn, rg_w, rg_b, re_w, re_b, wg, wu, wd):
    bsz, seq, d = hn.shape
    n_tok = bsz * seq
    t = hn.reshape(n_tok, d)
    g_logit = (t @ rg_w).astype(jnp.float32) + rg_b.astype(jnp.float32)
    g_prob = jax.nn.softmax(g_logit, axis=-1)
    grp = jnp.argmax(g_logit, axis=-1).astype(jnp.int32)
    e_logit = ((t @ re_w).astype(jnp.float32) + re_b.astype(jnp.float32)).reshape(n_tok, N_GROUPS, EXPERTS_PER_GROUP)
    e_sel = jnp.take_along_axis(e_logit, grp[:, None, None], axis=1)[:, 0]
    top_v, top_i = lax.top_k(e_sel, TOP_K)
    gate = jnp.take_along_axis(g_prob, grp[:, None], axis=1) * jax.nn.softmax(top_v, axis=-1)
    eidx = grp[:, None] * EXPERTS_PER_GROUP + top_i.astype(jnp.int32)
    y = routed_experts(t, eidx, gate.astype(t.dtype), wg, wu, wd)
    return y.reshape(bsz, seq, d)


def setup_inputs(seed: int = 0) -> dict:
    key = jax.random.key(seed)
    ks = iter(jax.random.split(key, 48))
    nrm = lambda shape, scale: jax.random.normal(next(ks), shape, jnp.float32) * scale
    uni = lambda shape, lo, hi: jax.random.uniform(next(ks), shape, jnp.float32, lo, hi)
    gain = lambda shape: 1.0 + nrm(shape, 0.02)
    D, NA, NB, L = D_MODEL, N_A_LAYERS, N_B_LAYERS, DEPTH
    sd = D ** -0.5
    return {
        'x': nrm((BATCH, SEQ, D), 1.0),
        'rwkv_norm_g': gain((NA, D)),
        'rwkv_mix': uni((NA, N_MIX, D), 0.0, 1.0),
        'rwkv_w_r': nrm((NA, D, D), sd),
        'rwkv_w_k': nrm((NA, D, D), sd),
        'rwkv_w_v': nrm((NA, D, D), sd),
        'rwkv_w0': uni((NA, D), -3.0, 1.0),
        'rwkv_w1': nrm((NA, D, DECAY_LORA), sd),
        'rwkv_w2': nrm((NA, DECAY_LORA, D), 0.1 * DECAY_LORA ** -0.5),
        'rwkv_a0': nrm((NA, D), 0.5),
        'rwkv_a1': nrm((NA, D, AAA_LORA), sd),
        'rwkv_a2': nrm((NA, AAA_LORA, D), 0.5 * AAA_LORA ** -0.5),
        'rwkv_g1': nrm((NA, D, GATE_LORA), sd),
        'rwkv_g2': nrm((NA, GATE_LORA, D), GATE_LORA ** -0.5),
        'rwkv_k_k': 0.85 + nrm((NA, D), 0.02),
        'rwkv_k_a': gain((NA, D)),
        'rwkv_r_k': nrm((NA, RWKV_HEADS, RWKV_HEAD), 0.1),
        'rwkv_gn_g': gain((NA, D)),
        'rwkv_gn_b': nrm((NA, D), 0.01),
        'rwkv_w_o': nrm((NA, D, D), sd),
        'kv_norm_g': gain((D,)),
        'w_kv': nrm((D, 2 * D), sd),
        'k_norm_g': gain((DIFF_HEAD,)),
        'attn_norm_g': gain((NB, D)),
        'attn_w_q': nrm((NB, D, D), sd),
        'q_norm_g': gain((NB, DIFF_HEAD)),
        'lambda_q1': nrm((NB, DIFF_HEAD), 0.1),
        'lambda_k1': nrm((NB, DIFF_HEAD), 0.1),
        'lambda_q2': nrm((NB, DIFF_HEAD), 0.1),
        'lambda_k2': nrm((NB, DIFF_HEAD), 0.1),
        'subln_g': gain((NB, 2 * DIFF_HEAD)),
        'attn_w_o': nrm((NB, D, D), sd),
        'moe_norm_g': gain((L, D)),
        'router_group_w': nrm((L, D, N_GROUPS), sd),
        'router_group_b': nrm((L, N_GROUPS), 0.01),
        'router_expert_w': nrm((L, D, N_EXPERTS), sd),
        'router_expert_b': nrm((L, N_EXPERTS), 0.01),
        'expert_w_gate': nrm((L, N_EXPERTS, D, EXPERT_HIDDEN), sd),
        'expert_w_up': nrm((L, N_EXPERTS, D, EXPERT_HIDDEN), sd),
        'expert_w_down': nrm((L, N_EXPERTS, EXPERT_HIDDEN, D), EXPERT_HIDDEN ** -0.5),
    }


def reference(x, rwkv_norm_g, rwkv_mix, rwkv_w_r, rwkv_w_k, rwkv_w_v, rwkv_w0, rwkv_w1, rwkv_w2,
              rwkv_a0, rwkv_a1, rwkv_a2, rwkv_g1, rwkv_g2, rwkv_k_k, rwkv_k_a, rwkv_r_k,
              rwkv_gn_g, rwkv_gn_b, rwkv_w_o, kv_norm_g, w_kv, k_norm_g, attn_norm_g, attn_w_q,
              q_norm_g, lambda_q1, lambda_k1, lambda_q2, lambda_k2, subln_g, attn_w_o,
              moe_norm_g, router_group_w, router_group_b, router_expert_w, router_expert_b,
              expert_w_gate, expert_w_up, expert_w_down):
    h = x
    k_shared = None
    v_shared = None
    for l in range(DEPTH):
        if l < N_A_LAYERS:
            i = l
            h = h + rwkv7_time_mix(
                rms_norm(h, rwkv_norm_g[i]), rwkv_mix[i], rwkv_w_r[i], rwkv_w_k[i], rwkv_w_v[i],
                rwkv_w0[i], rwkv_w1[i], rwkv_w2[i], rwkv_a0[i], rwkv_a1[i], rwkv_a2[i],
                rwkv_g1[i], rwkv_g2[i], rwkv_k_k[i], rwkv_k_a[i], rwkv_r_k[i],
                rwkv_gn_g[i], rwkv_gn_b[i], rwkv_w_o[i])
        else:
            j = l - N_A_LAYERS
            if j == 0:
                k_shared, v_shared = shared_kv(h, kv_norm_g, w_kv, k_norm_g)
            h = h + diff_attention(
                rms_norm(h, attn_norm_g[j]), k_shared, v_shared, attn_w_q[j], q_norm_g[j],
                lambda_q1[j], lambda_k1[j], lambda_q2[j], lambda_k2[j], subln_g[j], attn_w_o[j], l)
        h = h + hier_moe(rms_norm(h, moe_norm_g[l]), router_group_w[l], router_group_b[l],
                         router_expert_w[l], router_expert_b[l], expert_w_gate[l],
                         expert_w_up[l], expert_w_down[l])
    return h
```

```python
import functools
import math

import jax
import jax.numpy as jnp
from jax import lax
from jax.experimental import pallas as pl
from jax.experimental.pallas import tpu as pltpu

V7X_LANES = 128
V7X_SUBLANES = 8
V7X_VMEM_BYTES = 64 * 1024 * 1024
VMEM_LIMIT_BYTES = V7X_VMEM_BYTES - 8 * 1024 * 1024

RWKV_HEAD = 64
DIFF_HEAD = 64
N_GROUPS = 4
TOP_K = 2
GN_EPS = 64e-5
RMS_EPS = 1e-6
WKV_CHUNK = 64

BF16 = jnp.bfloat16
F32 = jnp.float32


def _tile(n, pref):
    t = min(n, pref)
    assert n % t == 0, (n, pref)
    return t


def _params(*sem):
    return pltpu.CompilerParams(dimension_semantics=sem, vmem_limit_bytes=VMEM_LIMIT_BYTES)


def _mm(a, b):
    return jnp.dot(a.astype(BF16), b.astype(BF16), preferred_element_type=F32)


def _mm_nt(a, b):
    return lax.dot_general(a.astype(BF16), b.astype(BF16), (((1,), (1,)), ((), ())), preferred_element_type=F32)


def _split_dot(x, w_bf16):
    hi = x.astype(BF16)
    lo = (x - hi.astype(F32)).astype(BF16)
    return (jnp.dot(hi, w_bf16, preferred_element_type=F32) + jnp.dot(lo, w_bf16, preferred_element_type=F32))


def _seg_ones(n, seg):
    i = lax.broadcasted_iota(jnp.int32, (n, n), 0) // seg
    j = lax.broadcasted_iota(jnp.int32, (n, n), 1) // seg
    return jnp.where(i == j, 1.0, 0.0).astype(BF16)


def _rms(x, g):
    return x * lax.rsqrt(jnp.mean(x * x, axis=-1, keepdims=True) + RMS_EPS) * g


def _rmsnorm_kernel(x_ref, g_ref, o_ref):
    o_ref[...] = _rms(x_ref[...], g_ref[...]).astype(o_ref.dtype)


def rmsnorm(x, g, out_dtype=BF16):
    n, d = x.shape
    tm = _tile(n, 512)
    return pl.pallas_call(
        _rmsnorm_kernel,
        grid=(n // tm,),
        in_specs=[pl.BlockSpec((tm, d), lambda i: (i, 0)), pl.BlockSpec((1, d), lambda i: (0, 0))],
        out_specs=pl.BlockSpec((tm, d), lambda i: (i, 0)),
        out_shape=jax.ShapeDtypeStruct((n, d), out_dtype),
        compiler_params=_params("parallel"),
        name="rmsnorm",
    )(x, g.reshape(1, d))


def _rwkv_prep_kernel(x_ref, prev_ref, g_ref, mix_ref, o_ref, *, tm, seq):
    i = pl.program_id(0)
    g = g_ref[...]
    xn = _rms(x_ref[...], g)
    prev = _rms(prev_ref[...], g)[V7X_SUBLANES - 1:V7X_SUBLANES]
    prev = jnp.where((i * tm) % seq == 0, 0.0, prev)
    row = lax.broadcasted_iota(jnp.int32, xn.shape, 0)
    shifted = jnp.where(row == 0, prev, pltpu.roll(xn, 1, 0))
    xx = shifted - xn
    for s in range(o_ref.shape[0]):
        o_ref[s] = (xn + xx * mix_ref[s:s + 1, :]).astype(o_ref.dtype)


def rwkv_prep(h, g, mix, seq):
    n, d = h.shape
    n_mix = mix.shape[0]
    tm = _tile(seq, 256)
    per = tm // V7X_SUBLANES
    return pl.pallas_call(
        functools.partial(_rwkv_prep_kernel, tm=tm, seq=seq),
        grid=(n // tm,),
        in_specs=[
            pl.BlockSpec((tm, d), lambda i: (i, 0)),
            pl.BlockSpec((V7X_SUBLANES, d), lambda i: (jnp.maximum(i * per - 1, 0), 0)),
            pl.BlockSpec((1, d), lambda i: (0, 0)),
            pl.BlockSpec((n_mix, d), lambda i: (0, 0)),
        ],
        out_specs=pl.BlockSpec((n_mix, tm, d), lambda i: (0, i, 0)),
        out_shape=jax.ShapeDtypeStruct((n_mix, n, d), BF16),
        compiler_params=_params("parallel"),
        name="rwkv_prep",
    )(h, h, g.reshape(1, d), mix)


def _matmul_kernel(x_ref, w_ref, *rest, epilogue, n_extra):
    extra = [r[...] for r in rest[:n_extra]]
    o_ref = rest[n_extra]
    acc = jnp.dot(x_ref[...], w_ref[...], preferred_element_type=F32)
    o_ref[...] = epilogue(acc, *extra).astype(o_ref.dtype)


def matmul(x, w, *, out_dtype, epilogue=None, rows=(), tiles=(), x_sel=None, tm=1024, tn=1024, name="matmul"):
    if x_sel is None:
        m, k = x.shape
        x_spec = lambda tm_: pl.BlockSpec((tm_, k), lambda i, j: (i, 0))
    else:
        _, m, k = x.shape
        x_spec = lambda tm_: pl.BlockSpec((None, tm_, k), lambda i, j: (x_sel, i, 0))
    n_out = w.shape[1]
    tm = _tile(m, tm)
    tn = _tile(n_out, tn)
    if epilogue is None:
        epilogue = lambda acc: acc
    in_specs = [x_spec(tm), pl.BlockSpec((k, tn), lambda i, j: (0, j))]
    in_specs += [pl.BlockSpec((1, tn), lambda i, j: (0, j)) for _ in rows]
    in_specs += [pl.BlockSpec((tm, tn), lambda i, j: (i, j)) for _ in tiles]
    return pl.pallas_call(
        functools.partial(_matmul_kernel, epilogue=epilogue, n_extra=len(rows) + len(tiles)),
        grid=(m // tm, n_out // tn),
        in_specs=in_specs,
        out_specs=pl.BlockSpec((tm, tn), lambda i, j: (i, j)),
        out_shape=jax.ShapeDtypeStruct((m, n_out), out_dtype),
        compiler_params=_params("parallel", "arbitrary"),
        name=name,
    )(x, w, *[r.reshape(1, n_out).astype(F32) for r in rows], *tiles)


def _seg_rms_epilogue(acc, g, *, seg, scale):
    ones = _seg_ones(V7X_LANES, seg)
    outs = []
    for s in range(acc.shape[1] // V7X_LANES):
        a = acc[:, s * V7X_LANES:(s + 1) * V7X_LANES]
        ss = _split_dot(a * a, ones)
        outs.append(a * lax.rsqrt(ss * (1.0 / seg) + RMS_EPS))
    return jnp.concatenate(outs, axis=1) * (g * scale)


def _wkv_kernel(r_ref, lw_ref, k_ref, v_ref, a_ref, g_ref, kk_ref, ka_ref, rk_ref, gg_ref, gb_ref, o_ref, h_ref,
                *, chunk, pairs):
    c2 = 2 * chunk
    half = V7X_LANES // 2

    @pl.when(pl.program_id(2) == 0)
    def _():
        h_ref[...] = jnp.zeros_like(h_ref)

    lane = lax.broadcasted_iota(jnp.int32, (chunk, V7X_LANES), 1)
    head0 = lane < half
    ri = lax.broadcasted_iota(jnp.int32, (c2, c2), 0)
    ci = lax.broadcasted_iota(jnp.int32, (c2, c2), 1)
    same = (ri // chunk) == (ci // chunk)
    strict = jnp.logical_and(same, (ci % chunk) < (ri % chunk))
    incl = jnp.logical_and(same, (ci % chunk) <= (ri % chunk))
    eye = jnp.where(ri == ci, 1.0, 0.0)
    li = lax.broadcasted_iota(jnp.int32, (chunk, chunk), 0)
    lj = lax.broadcasted_iota(jnp.int32, (chunk, chunk), 1)
    ltri = jnp.where(li >= lj, 1.0, 0.0).astype(BF16)
    seg = _seg_ones(V7X_LANES, RWKV_HEAD)
    n_double = int(math.log2(chunk)) - 1

    def stack(x):
        return jnp.concatenate([jnp.where(head0, x, 0.0), jnp.where(head0, 0.0, x)], axis=0)

    def one_pair(rows, p):
        cols = slice(p * V7X_LANES, (p + 1) * V7X_LANES)
        r = r_ref[rows, cols]
        lw = lw_ref[rows, cols]
        k = k_ref[rows, cols]
        v = v_ref[rows, cols]
        a = a_ref[rows, cols]
        kk = k * kk_ref[:, cols]
        kk = kk * lax.rsqrt(jnp.maximum(_split_dot(kk * kk, seg), 1e-24))
        k = k * (1.0 + (a - 1.0) * ka_ref[:, cols])
        b = kk * a
        a_in = -kk

        lw_hi = lw.astype(BF16)
        lw_lo = (lw - lw_hi.astype(F32)).astype(BF16)
        cum = jnp.dot(ltri, lw_hi, preferred_element_type=F32) + jnp.dot(ltri, lw_lo, preferred_element_type=F32)
        cprev = cum - lw
        mid = cum[chunk // 2 - 1:chunk // 2]
        last = cum[chunk - 1:chunk]
        dec_mid = jnp.exp(mid - cum)
        dec_last = jnp.exp(last - cum)
        As = stack(a_in * jnp.exp(cprev - mid))
        Rs = stack(r * jnp.exp(cum - mid))
        Bs = stack(b * dec_mid)
        Ks = stack(k * dec_mid)
        Vs = stack(v)
        Bh = stack(b * dec_last)
        Kh = stack(k * dec_last)
        emid = jnp.exp(mid)
        plast = jnp.exp(last)

        AR = jnp.concatenate([As, Rs], axis=0)
        BK = jnp.concatenate([Bs, Ks], axis=0)
        G = _mm_nt(AR, BK)
        Aab = jnp.where(strict, G[:c2, :c2], 0.0)
        Aak = jnp.where(strict, G[:c2, c2:], 0.0)
        Srb = jnp.where(incl, G[c2:, :c2], 0.0)
        Srk = jnp.where(incl, G[c2:, c2:], 0.0)

        X = eye + Aab
        Pw = Aab
        for _ in range(n_double):
            Pw = _mm(Pw, Pw)
            X = X + _mm(Pw, X)
        AV = _mm(Aak, Vs)
        TA = _mm(X, jnp.concatenate([As, AV], axis=1))
        Abar_m = TA[:, :V7X_LANES]
        Ubar = TA[:, V7X_LANES:]
        SA = _mm(Srb, TA)
        Rbar = (Rs + SA[:, :V7X_LANES]) * emid
        Y0 = SA[:, V7X_LANES:] + _mm(Srk, Vs)
        Abar = Abar_m * emid

        H = h_ref[p]
        Ys = _mm(Rbar, H) + Y0
        y = Ys[:chunk] + Ys[chunk:]

        BhT = jnp.transpose(Bh)
        KhT = jnp.transpose(Kh)
        MN = _mm(BhT, jnp.concatenate([Abar, Ubar], axis=1))
        M = eye * plast + MN[:, :V7X_LANES]
        Nn = MN[:, V7X_LANES:] + _mm(KhT, Vs)
        h_ref[p] = _mm(M, H) + Nn

        inv = 1.0 / RWKV_HEAD
        mean = _split_dot(y, seg) * inv
        yc = y - mean
        var = _split_dot(yc * yc, seg) * inv
        yn = yc * lax.rsqrt(var + GN_EPS) * gg_ref[:, cols] + gb_ref[:, cols]
        bonus = _split_dot(r * k * rk_ref[:, cols], seg) * v
        o_ref[rows, cols] = ((yn + bonus) * g_ref[rows, cols]).astype(o_ref.dtype)

    def body(c, carry):
        rows = pl.ds(pl.multiple_of(c * chunk, chunk), chunk)
        for p in range(pairs):
            one_pair(rows, p)
        return carry

    lax.fori_loop(0, r_ref.shape[0] // chunk, body, 0)


def wkv7(r, lw, k, v, a, g, k_k, k_a, r_k, gn_g, gn_b, batch, seq):
    n, d = r.shape
    pairs = 2
    width = pairs * V7X_LANES
    tb = _tile(seq, 512)
    seq_spec = pl.BlockSpec((tb, width), lambda b, p, t: (b * (seq // tb) + t, p))
    par_spec = pl.BlockSpec((1, width), lambda b, p, t: (0, p))
    row = lambda x: x.reshape(1, d).astype(F32)
    return pl.pallas_call(
        functools.partial(_wkv_kernel, chunk=WKV_CHUNK, pairs=pairs),
        grid=(batch, d // width, seq // tb),
        in_specs=[seq_spec] * 6 + [par_spec] * 5,
        out_specs=seq_spec,
        out_shape=jax.ShapeDtypeStruct((n, d), BF16),
        scratch_shapes=[pltpu.VMEM((pairs, V7X_LANES, V7X_LANES), F32)],
        compiler_params=_params("parallel", "parallel", "arbitrary"),
        name="wkv7",
    )(r, lw, k, v, a, g, row(k_k), row(k_a), row(r_k), row(gn_g), row(gn_b))


def _decay_epilogue(acc, w0):
    z = -(w0 + acc)
    softplus = jnp.maximum(z, 0.0) + jnp.log(1.0 + jnp.exp(-jnp.abs(z)))
    return -jnp.exp(-softplus - 0.5)


def rwkv_layer(h, batch, seq, norm_g, mix, w_r, w_k, w_v, w0, w1, w2, a0, a1, a2, g1, g2, k_k, k_a, r_k, gn_g,
               gn_b, w_o):
    d = h.shape[1]
    xs = rwkv_prep(h, norm_g, mix, seq)
    bf = lambda w: w.astype(BF16)

    def pad_cols(w):
        return jnp.pad(w, ((0, 0), (0, -w.shape[1] % V7X_LANES)))

    def pad_rows(w):
        return jnp.pad(w, ((0, -w.shape[0] % V7X_LANES), (0, 0)))

    r = matmul(xs, bf(w_r), x_sel=0, out_dtype=F32, name="rwkv_r")
    k = matmul(xs, bf(w_k), x_sel=2, out_dtype=F32, name="rwkv_k")
    v = matmul(xs, bf(w_v), x_sel=3, out_dtype=F32, name="rwkv_v")
    tw = matmul(xs, bf(pad_cols(w1)), x_sel=1, out_dtype=BF16, epilogue=jnp.tanh, name="rwkv_w1")
    lw = matmul(tw, bf(pad_rows(w2)), out_dtype=F32, rows=(w0,), name="rwkv_w2",
                epilogue=_decay_epilogue)
    ta = matmul(xs, bf(pad_cols(a1)), x_sel=4, out_dtype=BF16, name="rwkv_a1")
    a = matmul(ta, bf(pad_rows(a2)), out_dtype=F32, rows=(a0,), name="rwkv_a2",
               epilogue=lambda acc, b: jax.nn.sigmoid(b + acc))
    tg = matmul(xs, bf(pad_cols(g1)), x_sel=5, out_dtype=BF16, epilogue=jax.nn.sigmoid, name="rwkv_g1")
    g = matmul(tg, bf(pad_rows(g2)), out_dtype=F32, name="rwkv_g2")
    y = wkv7(r, lw, k, v, a, g, k_k, k_a, r_k, gn_g, gn_b, batch, seq)
    return matmul(y, bf(w_o), out_dtype=F32, tiles=(h,), epilogue=lambda acc, res: res + acc, name="rwkv_o")


def _attn_kernel(lam_ref, q_ref, k_ref, v_ref, sg_ref, o_ref, *, tq, out_scale):
    qi = pl.program_id(2)
    q = q_ref[...]
    lane = lax.broadcasted_iota(jnp.int32, q.shape, 1)
    zero = jnp.zeros_like(q)
    qs = (jnp.where(lane < DIFF_HEAD, q, zero), jnp.where(lane < DIFF_HEAD, zero, q))
    row = lax.broadcasted_iota(jnp.int32, (tq, tq), 0)
    col = lax.broadcasted_iota(jnp.int32, (tq, tq), 1)
    causal = row >= col

    def step(j, carry, masked):
        start = pl.multiple_of(j * tq, tq)
        kj = k_ref[pl.ds(start, tq), :]
        vj = v_ref[pl.ds(start, tq), :]
        out = []
        for sub in range(2):
            m, l, acc = carry[sub]
            s = lax.dot_general(qs[sub], kj, (((1,), (1,)), ((), ())), preferred_element_type=F32)
            if masked:
                s = jnp.where(causal, s, -1e30)
            m_new = jnp.maximum(m, jnp.max(s, axis=-1, keepdims=True))
            alpha = jnp.exp(m - m_new)
            p = jnp.exp(s - m_new)
            l = alpha * l + jnp.sum(p, axis=-1, keepdims=True)
            acc = alpha * acc + jnp.dot(p.astype(BF16), vj, preferred_element_type=F32)
            out.append((m_new, l, acc))
        return tuple(out)

    init = tuple((jnp.full((tq, 1), -1e30, F32), jnp.zeros((tq, 1), F32), jnp.zeros((tq, V7X_LANES), F32))
                 for _ in range(2))
    carry = lax.fori_loop(0, qi, lambda j, c: step(j, c, False), init)
    (_, l0, acc0), (_, l1, acc1) = step(qi, carry, True)
    o = acc0 / l0 - lam_ref[0] * (acc1 / l1)
    o_ref[...] = (_rms(o, sg_ref[...]) * out_scale).astype(o_ref.dtype)


def diff_attention_core(q, k, v, lam, subln_g, batch, seq, lam_init):
    n, d = q.shape
    tq = _tile(seq, 512)
    nq = seq // tq
    head_w = 2 * DIFF_HEAD
    return pl.pallas_call(
        functools.partial(_attn_kernel, tq=tq, out_scale=1.0 - lam_init),
        grid=(batch, d // head_w, nq),
        in_specs=[
            pl.BlockSpec(memory_space=pltpu.SMEM),
            pl.BlockSpec((tq, head_w), lambda b, h, i: (b * nq + i, h)),
            pl.BlockSpec((seq, head_w), lambda b, h, i: (b, h)),
            pl.BlockSpec((seq, head_w), lambda b, h, i: (b, h)),
            pl.BlockSpec((1, head_w), lambda b, h, i: (0, 0)),
        ],
        out_specs=pl.BlockSpec((tq, head_w), lambda b, h, i: (b * nq + i, h)),
        out_shape=jax.ShapeDtypeStruct((n, d), BF16),
        compiler_params=_params("parallel", "parallel", "arbitrary"),
        name="diff_attn",
    )(lam.reshape(1).astype(F32), q, k, v, subln_g.reshape(1, head_w).astype(F32))


def shared_kv(h, kv_norm_g, w_kv, k_norm_g):
    d = h.shape[1]
    hn = rmsnorm(h, kv_norm_g)
    kg = jnp.tile(k_norm_g, d // DIFF_HEAD)
    k = matmul(hn, w_kv[:, :d].astype(BF16), out_dtype=BF16, rows=(kg,), name="kv_k", tn=512,
               epilogue=functools.partial(_seg_rms_epilogue, seg=DIFF_HEAD, scale=1.0))
    v = matmul(hn, w_kv[:, d:].astype(BF16), out_dtype=BF16, name="kv_v")
    return k, v


def attn_layer(h, k, v, batch, seq, layer_idx, norm_g, w_q, q_norm_g, lq1, lk1, lq2, lk2, subln_g, w_o):
    d = h.shape[1]
    hn = rmsnorm(h, norm_g)
    qg = jnp.tile(q_norm_g, d // DIFF_HEAD)
    q = matmul(hn, w_q.astype(BF16), out_dtype=BF16, rows=(qg,), name="attn_q", tn=512,
               epilogue=functools.partial(_seg_rms_epilogue, seg=DIFF_HEAD, scale=DIFF_HEAD ** -0.5))
    lam_init = 0.8 - 0.6 * math.exp(-0.3 * layer_idx)
    lam = jnp.exp(jnp.sum(lq1 * lk1)) - jnp.exp(jnp.sum(lq2 * lk2)) + lam_init
    o = diff_attention_core(q, k, v, lam, subln_g, batch, seq, lam_init)
    return matmul(o, w_o.astype(BF16), out_dtype=F32, tiles=(h,), epilogue=lambda acc, res: res + acc,
                  name="attn_o")


def _router_kernel(x_ref, g_ref, w_ref, b_ref, t_ref, logit_ref):
    t = _rms(x_ref[...], g_ref[...])
    t_ref[...] = t
    logit_ref[...] = jnp.dot(t, w_ref[...], preferred_element_type=F32, precision=lax.Precision.HIGHEST) + b_ref[...]


def moe_router(h, norm_g, rg_w, rg_b, re_w, re_b):
    n, d = h.shape
    tm = _tile(n, 256)
    n_log = rg_w.shape[1] + re_w.shape[1]
    pad = -n_log % V7X_LANES
    w = jnp.pad(jnp.concatenate([rg_w, re_w], axis=1), ((0, 0), (0, pad)))
    b = jnp.pad(jnp.concatenate([rg_b, re_b]), (0, pad)).reshape(1, -1)
    wl = w.shape[1]
    return pl.pallas_call(
        _router_kernel,
        grid=(n // tm,),
        in_specs=[pl.BlockSpec((tm, d), lambda i: (i, 0)), pl.BlockSpec((1, d), lambda i: (0, 0)),
                  pl.BlockSpec((d, wl), lambda i: (0, 0)), pl.BlockSpec((1, wl), lambda i: (0, 0))],
        out_specs=[pl.BlockSpec((tm, d), lambda i: (i, 0)), pl.BlockSpec((tm, wl), lambda i: (i, 0))],
        out_shape=[jax.ShapeDtypeStruct((n, d), F32), jax.ShapeDtypeStruct((n, wl), F32)],
        compiler_params=_params("parallel"),
        name="moe_router",
    )(h, norm_g.reshape(1, d), w, b)


def _gather_kernel(idx_ref, x_hbm, o_ref, sem, *, rows):
    def copy(r, src_row):
        return pltpu.make_async_copy(x_hbm.at[pl.ds(src_row, 1), :], o_ref.at[pl.ds(r, 1), :], sem)

    def issue(r, c):
        copy(r, idx_ref[0, 0, r]).start()
        return c

    def drain(r, c):
        copy(r, 0).wait()
        return c

    lax.fori_loop(0, rows, issue, 0)
    lax.fori_loop(0, rows, drain, 0)


def gather_rows(x, idx, rows_per_step=256):
    n_out = idx.shape[0]
    d = x.shape[1]
    bm = _tile(n_out, rows_per_step)
    return pl.pallas_call(
        functools.partial(_gather_kernel, rows=bm),
        grid=(n_out // bm,),
        in_specs=[pl.BlockSpec((1, 1, bm), lambda i: (i, 0, 0), memory_space=pltpu.SMEM),
                  pl.BlockSpec(memory_space=pl.ANY)],
        out_specs=pl.BlockSpec((bm, d), lambda i: (i, 0)),
        out_shape=jax.ShapeDtypeStruct((n_out, d), x.dtype),
        scratch_shapes=[pltpu.SemaphoreType.DMA(())],
        compiler_params=_params("arbitrary"),
        name="moe_gather",
    )(idx.reshape(n_out // bm, 1, bm), x)


def _expert_kernel(be_ref, nused_ref, x_ref, wg_ref, wu_ref, wd_ref, rw_ref, o_ref):
    i = pl.program_id(0)

    @pl.when(i < nused_ref[0])
    def _():
        x = x_ref[...].astype(BF16)
        hg = jnp.dot(x, wg_ref[...], preferred_element_type=F32)
        hu = jnp.dot(x, wu_ref[...], preferred_element_type=F32)
        act = (hg * jax.nn.sigmoid(hg) * hu).astype(BF16)
        o_ref[...] = jnp.dot(act, wd_ref[...], preferred_element_type=F32) * rw_ref[...]

    @pl.when(i >= nused_ref[0])
    def _():
        o_ref[...] = jnp.zeros_like(o_ref)


def moe_experts(xb, block_e, n_used, row_w, wg, wu, wd, bm):
    n_pad, d = xb.shape
    hid = wg.shape[2]
    return pl.pallas_call(
        _expert_kernel,
        grid_spec=pltpu.PrefetchScalarGridSpec(
            num_scalar_prefetch=2,
            grid=(n_pad // bm,),
            in_specs=[
                pl.BlockSpec((bm, d), lambda i, be, nu: (i, 0)),
                pl.BlockSpec((None, d, hid), lambda i, be, nu: (be[i], 0, 0)),
                pl.BlockSpec((None, d, hid), lambda i, be, nu: (be[i], 0, 0)),
                pl.BlockSpec((None, hid, d), lambda i, be, nu: (be[i], 0, 0)),
                pl.BlockSpec((bm, 1), lambda i, be, nu: (i, 0)),
            ],
            out_specs=pl.BlockSpec((bm, d), lambda i, be, nu: (i, 0)),
        ),
        out_shape=jax.ShapeDtypeStruct((n_pad, d), F32),
        compiler_params=_params("arbitrary"),
        name="moe_experts",
    )(block_e, n_used, xb, wg, wu, wd, row_w.reshape(n_pad, 1))


def _combine_kernel(idx_ref, yb_hbm, h_ref, o_ref, buf, sem, *, rows):
    def copy(r, s, src_row):
        return pltpu.make_async_copy(yb_hbm.at[pl.ds(src_row, 1), :], buf.at[s, pl.ds(r, 1), :], sem)

    def issue(r, c):
        for s in range(TOP_K):
            copy(r, s, idx_ref[0, s, r]).start()
        return c

    def drain(r, c):
        for s in range(TOP_K):
            copy(r, s, 0).wait()
        return c

    lax.fori_loop(0, rows, issue, 0)
    lax.fori_loop(0, rows, drain, 0)
    acc = h_ref[...]
    for s in range(TOP_K):
        acc = acc + buf[s]
    o_ref[...] = acc


def moe_combine(h, yb, pos, rows_per_step=256):
    n, d = h.shape
    tc = _tile(n, rows_per_step)
    idx = pos.reshape(n // tc, tc, TOP_K).transpose(0, 2, 1)
    return pl.pallas_call(
        functools.partial(_combine_kernel, rows=tc),
        grid=(n // tc,),
        in_specs=[pl.BlockSpec((1, TOP_K, tc), lambda i: (i, 0, 0), memory_space=pltpu.SMEM),
                  pl.BlockSpec(memory_space=pl.ANY),
                  pl.BlockSpec((tc, d), lambda i: (i, 0))],
        out_specs=pl.BlockSpec((tc, d), lambda i: (i, 0)),
        out_shape=jax.ShapeDtypeStruct((n, d), F32),
        scratch_shapes=[pltpu.VMEM((TOP_K, tc, d), F32), pltpu.SemaphoreType.DMA(())],
        compiler_params=_params("arbitrary"),
        name="moe_combine",
    )(idx, yb, h)


def moe_layer(h, norm_g, rg_w, rg_b, re_w, re_b, wg, wu, wd, bm=256):
    n, d = h.shape
    n_exp = wg.shape[0]
    epg = n_exp // N_GROUPS
    t, logits = moe_router(h, norm_g, rg_w, rg_b, re_w, re_b)
    g_logit = logits[:, :N_GROUPS]
    e_logit = logits[:, N_GROUPS:N_GROUPS + n_exp].reshape(n, N_GROUPS, epg)
    g_prob = jax.nn.softmax(g_logit, axis=-1)
    grp = jnp.argmax(g_logit, axis=-1).astype(jnp.int32)
    e_sel = jnp.take_along_axis(e_logit, grp[:, None, None], axis=1)[:, 0]
    top_v, top_i = lax.top_k(e_sel, TOP_K)
    gate = jnp.take_along_axis(g_prob, grp[:, None], axis=1) * jax.nn.softmax(top_v, axis=-1)
    eidx = grp[:, None] * epg + top_i.astype(jnp.int32)

    n_assign = n * TOP_K
    n_pad = n_assign + n_exp * bm
    flat_e = eidx.reshape(-1)
    order = jnp.argsort(flat_e).astype(jnp.int32)
    sorted_e = flat_e[order]
    counts = jnp.bincount(flat_e, length=n_exp).astype(jnp.int32)
    padded = (counts + bm - 1) // bm * bm
    padded_end = jnp.cumsum(padded)
    padded_start = padded_end - padded
    start = jnp.cumsum(counts) - counts
    dest = padded_start[sorted_e] + jnp.arange(n_assign, dtype=jnp.int32) - start[sorted_e]
    buf_tok = jnp.zeros((n_pad,), jnp.int32).at[dest].set(order // TOP_K)
    buf_w = jnp.zeros((n_pad,), F32).at[dest].set(gate.reshape(-1)[order])
    pos = jnp.zeros((n_assign,), jnp.int32).at[order].set(dest).reshape(n, TOP_K)
    n_blocks = n_pad // bm
    block_start = jnp.arange(n_blocks, dtype=jnp.int32) * bm
    block_e = jnp.minimum(jnp.searchsorted(padded_end, block_start, side="right"), n_exp - 1).astype(jnp.int32)
    n_used = (padded_end[-1] // bm).astype(jnp.int32).reshape(1)
    block_e = jnp.where(jnp.arange(n_blocks) < n_used[0], block_e, block_e[jnp.maximum(n_used[0] - 1, 0)])

    xb = gather_rows(t, buf_tok, bm)
    yb = moe_experts(xb, block_e, n_used, buf_w, wg.astype(BF16), wu.astype(BF16), wd.astype(BF16), bm)
    return moe_combine(h, yb, pos)


def kernel(x, rwkv_norm_g, rwkv_mix, rwkv_w_r, rwkv_w_k, rwkv_w_v, rwkv_w0, rwkv_w1, rwkv_w2, rwkv_a0, rwkv_a1,
           rwkv_a2, rwkv_g1, rwkv_g2, rwkv_k_k, rwkv_k_a, rwkv_r_k, rwkv_gn_g, rwkv_gn_b, rwkv_w_o, kv_norm_g,
           w_kv, k_norm_g, attn_norm_g, attn_w_q, q_norm_g, lambda_q1, lambda_k1, lambda_q2, lambda_k2, subln_g,
           attn_w_o, moe_norm_g, router_group_w, router_group_b, router_expert_w, router_expert_b, expert_w_gate,
           expert_w_up, expert_w_down):
    batch, seq, d = x.shape
    depth = moe_norm_g.shape[0]
    n_a = rwkv_norm_g.shape[0]
    h = x.reshape(batch * seq, d)
    k_shared = v_shared = None
    for l in range(depth):
        if l < n_a:
            i = l
            h = rwkv_layer(h, batch, seq, rwkv_norm_g[i], rwkv_mix[i], rwkv_w_r[i], rwkv_w_k[i], rwkv_w_v[i],
                           rwkv_w0[i], rwkv_w1[i], rwkv_w2[i], rwkv_a0[i], rwkv_a1[i], rwkv_a2[i], rwkv_g1[i],
                           rwkv_g2[i], rwkv_k_k[i], rwkv_k_a[i], rwkv_r_k[i], rwkv_gn_g[i], rwkv_gn_b[i],
                           rwkv_w_o[i])
        else:
            j = l - n_a
            if j == 0:
                k_shared, v_shared = shared_kv(h, kv_norm_g, w_kv, k_norm_g)
            h = attn_layer(h, k_shared, v_shared, batch, seq, l, attn_norm_g[j], attn_w_q[j], q_norm_g[j],
                           lambda_q1[j], lambda_k1[j], lambda_q2[j], lambda_k2[j], subln_g[j], attn_w_o[j])
        h = moe_layer(h, moe_norm_g[l], router_group_w[l], router_group_b[l], router_expert_w[l],
                      router_expert_b[l], expert_w_gate[l], expert_w_up[l], expert_w_down[l])
    return h.reshape(batch, seq, d)
```

```python
import functools
import math

import jax
import jax.numpy as jnp
from jax import lax
from jax.experimental import pallas as pl
from jax.experimental.pallas import tpu as pltpu

V7X_LANES = 128
V7X_SUBLANES = 8
V7X_VMEM_BYTES = 64 * 1024 * 1024
VMEM_LIMIT_BYTES = V7X_VMEM_BYTES - 8 * 1024 * 1024

RWKV_HEAD = 64
DIFF_HEAD = 64
N_GROUPS = 4
TOP_K = 2
GN_EPS = 64e-5
RMS_EPS = 1e-6
WKV_CHUNK = 64
WKV_PAIRS = 16

BF16 = jnp.bfloat16
F32 = jnp.float32


def _tile(n, pref):
    t = min(n, pref)
    assert n % t == 0, (n, pref)
    return t


def _params(*sem):
    return pltpu.CompilerParams(dimension_semantics=sem, vmem_limit_bytes=VMEM_LIMIT_BYTES)


def _mm(a, b):
    return jnp.dot(a.astype(BF16), b.astype(BF16), preferred_element_type=F32)


def _mm_nt(a, b):
    return lax.dot_general(a.astype(BF16), b.astype(BF16), (((1,), (1,)), ((), ())), preferred_element_type=F32)


def _split_dot(x, w_bf16):
    hi = x.astype(BF16)
    lo = (x - hi.astype(F32)).astype(BF16)
    return (jnp.dot(hi, w_bf16, preferred_element_type=F32) + jnp.dot(lo, w_bf16, preferred_element_type=F32))


def _seg_ones(n, seg):
    i = lax.broadcasted_iota(jnp.int32, (n, n), 0) // seg
    j = lax.broadcasted_iota(jnp.int32, (n, n), 1) // seg
    return jnp.where(i == j, 1.0, 0.0).astype(BF16)


def _rms(x, g):
    return x * lax.rsqrt(jnp.mean(x * x, axis=-1, keepdims=True) + RMS_EPS) * g


def _rmsnorm_kernel(x_ref, g_ref, o_ref):
    o_ref[...] = _rms(x_ref[...], g_ref[...]).astype(o_ref.dtype)


def rmsnorm(x, g, out_dtype=BF16):
    n, d = x.shape
    tm = _tile(n, 512)
    return pl.pallas_call(
        _rmsnorm_kernel,
        grid=(n // tm,),
        in_specs=[pl.BlockSpec((tm, d), lambda i: (i, 0)), pl.BlockSpec((1, d), lambda i: (0, 0))],
        out_specs=pl.BlockSpec((tm, d), lambda i: (i, 0)),
        out_shape=jax.ShapeDtypeStruct((n, d), out_dtype),
        compiler_params=_params("parallel"),
        name="rmsnorm",
    )(x, g.reshape(1, d))


def _rwkv_prep_kernel(x_ref, prev_ref, g_ref, mix_ref, o_ref, *, tm, seq):
    i = pl.program_id(0)
    g = g_ref[...]
    xn = _rms(x_ref[...], g)
    prev = _rms(prev_ref[...], g)[V7X_SUBLANES - 1:V7X_SUBLANES]
    prev = jnp.where((i * tm) % seq == 0, 0.0, prev)
    row = lax.broadcasted_iota(jnp.int32, xn.shape, 0)
    shifted = jnp.where(row == 0, prev, pltpu.roll(xn, 1, 0))
    xx = shifted - xn
    for s in range(o_ref.shape[0]):
        o_ref[s] = (xn + xx * mix_ref[s:s + 1, :]).astype(o_ref.dtype)


def rwkv_prep(h, g, mix, seq):
    n, d = h.shape
    n_mix = mix.shape[0]
    tm = _tile(seq, 256)
    per = tm // V7X_SUBLANES
    return pl.pallas_call(
        functools.partial(_rwkv_prep_kernel, tm=tm, seq=seq),
        grid=(n // tm,),
        in_specs=[
            pl.BlockSpec((tm, d), lambda i: (i, 0)),
            pl.BlockSpec((V7X_SUBLANES, d), lambda i: (jnp.maximum(i * per - 1, 0), 0)),
            pl.BlockSpec((1, d), lambda i: (0, 0)),
            pl.BlockSpec((n_mix, d), lambda i: (0, 0)),
        ],
        out_specs=pl.BlockSpec((n_mix, tm, d), lambda i: (0, i, 0)),
        out_shape=jax.ShapeDtypeStruct((n_mix, n, d), BF16),
        compiler_params=_params("parallel"),
        name="rwkv_prep",
    )(h, h, g.reshape(1, d), mix)


def _matmul_kernel(x_ref, w_ref, *rest, epilogue, n_extra):
    extra = [r[...] for r in rest[:n_extra]]
    o_ref = rest[n_extra]
    acc = jnp.dot(x_ref[...], w_ref[...], preferred_element_type=F32)
    o_ref[...] = epilogue(acc, *extra).astype(o_ref.dtype)


def matmul(x, w, *, out_dtype, epilogue=None, rows=(), tiles=(), x_sel=None, w_cols=None, tm=1024, tn=1024,
           name="matmul"):
    if x_sel is None:
        m, k = x.shape
        x_spec = lambda tm_: pl.BlockSpec((tm_, k), lambda i, j: (i, 0))
    else:
        _, m, k = x.shape
        x_spec = lambda tm_: pl.BlockSpec((None, tm_, k), lambda i, j: (x_sel, i, 0))
    col0, n_out = (0, w.shape[1]) if w_cols is None else w_cols
    tm = _tile(m, tm)
    tn = _tile(n_out, tn)
    assert col0 % tn == 0
    j0 = col0 // tn
    if epilogue is None:
        epilogue = lambda acc: acc
    in_specs = [x_spec(tm), pl.BlockSpec((k, tn), lambda i, j: (0, j + j0))]
    in_specs += [pl.BlockSpec((1, tn), lambda i, j: (0, j)) for _ in rows]
    in_specs += [pl.BlockSpec((tm, tn), lambda i, j: (i, j)) for _ in tiles]
    return pl.pallas_call(
        functools.partial(_matmul_kernel, epilogue=epilogue, n_extra=len(rows) + len(tiles)),
        grid=(m // tm, n_out // tn),
        in_specs=in_specs,
        out_specs=pl.BlockSpec((tm, tn), lambda i, j: (i, j)),
        out_shape=jax.ShapeDtypeStruct((m, n_out), out_dtype),
        compiler_params=_params("parallel", "arbitrary"),
        name=name,
    )(x, w, *[r.reshape(1, n_out).astype(F32) for r in rows], *tiles)


def _seg_rms_epilogue(acc, g, *, seg, scale):
    ones = _seg_ones(V7X_LANES, seg)
    outs = []
    for s in range(acc.shape[1] // V7X_LANES):
        a = acc[:, s * V7X_LANES:(s + 1) * V7X_LANES]
        ss = _split_dot(a * a, ones)
        outs.append(a * lax.rsqrt(ss * (1.0 / seg) + RMS_EPS))
    return jnp.concatenate(outs, axis=1) * (g * scale)


def _wkv_kernel(r_ref, lw_ref, k_ref, v_ref, a_ref, g_ref, kk_ref, ka_ref, rk_ref, gg_ref, gb_ref, o_ref, h_ref,
                *, chunk, pairs):
    c2 = 2 * chunk
    half = V7X_LANES // 2
    assert c2 == V7X_LANES, "two stacked heads of one chunk must fill one 128-row tile"

    @pl.when(pl.program_id(2) == 0)
    def _():
        h_ref[...] = jnp.zeros_like(h_ref)

    lane = lax.broadcasted_iota(jnp.int32, (chunk, V7X_LANES), 1)
    head0 = lane < half
    ri = lax.broadcasted_iota(jnp.int32, (c2, c2), 0)
    ci = lax.broadcasted_iota(jnp.int32, (c2, c2), 1)
    same = (ri // chunk) == (ci // chunk)
    strict = jnp.logical_and(same, (ci % chunk) < (ri % chunk))
    incl = jnp.logical_and(same, (ci % chunk) <= (ri % chunk))
    eye = jnp.where(ri == ci, 1.0, 0.0)
    li = lax.broadcasted_iota(jnp.int32, (chunk, chunk), 0)
    lj = lax.broadcasted_iota(jnp.int32, (chunk, chunk), 1)
    ltri = jnp.where(li >= lj, 1.0, 0.0).astype(BF16)
    seg = _seg_ones(V7X_LANES, RWKV_HEAD)
    n_double = int(math.log2(chunk)) - 1

    def stack(x):
        return jnp.concatenate([jnp.where(head0, x, 0.0), jnp.where(head0, 0.0, x)], axis=0)

    def hi_lo(x, axis):
        hi = x.astype(BF16)
        return jnp.concatenate([hi, (x - hi.astype(F32)).astype(BF16)], axis=axis)

    def seg_sums(xs):
        prods = [jnp.dot(hi_lo(x, 0), seg, preferred_element_type=F32) for x in xs]
        return [o[:chunk] + o[chunk:] for o in prods]

    lanes = V7X_LANES
    P = range(pairs)
    cols = [slice(p * lanes, (p + 1) * lanes) for p in P]
    zeros = jnp.zeros((c2, lanes), F32)

    def body(c, carry):
        rows = pl.ds(pl.multiple_of(c * chunk, chunk), chunk)
        ld = lambda ref: [ref[rows, cs].astype(F32) for cs in cols]
        par = lambda ref: [ref[:, cs] for cs in cols]
        r, lw, k, v, a = ld(r_ref), ld(lw_ref), ld(k_ref), ld(v_ref), ld(a_ref)
        k_k, k_a, r_k = par(kk_ref), par(ka_ref), par(rk_ref)

        kk = [k[p] * k_k[p] for p in P]
        ss = seg_sums([x * x for x in kk])
        kk = [kk[p] * lax.rsqrt(jnp.maximum(ss[p], 1e-24)) for p in P]
        k = [k[p] * (1.0 + (a[p] - 1.0) * k_a[p]) for p in P]
        b = [kk[p] * a[p] for p in P]
        bonus = seg_sums([r[p] * k[p] * r_k[p] for p in P])

        cc = [jnp.dot(ltri, hi_lo(lw[p], 1), preferred_element_type=F32) for p in P]
        cum = [x[:, :lanes] + x[:, lanes:] for x in cc]
        mid = [x[chunk // 2 - 1:chunk // 2] for x in cum]
        last = [x[chunk - 1:chunk] for x in cum]
        dec_mid = [jnp.exp(mid[p] - cum[p]) for p in P]
        dec_last = [jnp.exp(last[p] - cum[p]) for p in P]
        emid = [jnp.exp(x) for x in mid]
        plast = [jnp.exp(x) for x in last]
        As = [stack(-kk[p] * jnp.exp(cum[p] - lw[p] - mid[p])) for p in P]
        Rs = [stack(r[p] * jnp.exp(cum[p] - mid[p])) for p in P]
        Vs = [stack(v[p]) for p in P]
        AR = [jnp.concatenate([As[p], Rs[p]], axis=0) for p in P]
        BK = [jnp.concatenate([stack(b[p] * dec_mid[p]), stack(k[p] * dec_mid[p])], axis=0) for p in P]
        BKh = [jnp.concatenate([stack(b[p] * dec_last[p]), stack(k[p] * dec_last[p])], axis=0) for p in P]

        G = [_mm_nt(AR[p], BK[p]) for p in P]
        Aab = [jnp.where(strict, g[:c2, :c2], 0.0) for g in G]
        Aak = [jnp.where(strict, g[:c2, c2:], 0.0) for g in G]
        S = [jnp.concatenate([jnp.where(incl, g[c2:, :c2], 0.0), jnp.where(incl, g[c2:, c2:], 0.0)], axis=1)
             for g in G]

        X = [eye + x for x in Aab]
        Q = [_mm(x, x) for x in Aab]
        for _ in range(n_double - 1):
            QX = [_mm(Q[p], jnp.concatenate([Q[p], X[p]], axis=1)) for p in P]
            X = [X[p] + QX[p][:, lanes:] for p in P]
            Q = [x[:, :lanes] for x in QX]
        X = [X[p] + _mm(Q[p], X[p]) for p in P]

        AV = [_mm(Aak[p], Vs[p]) for p in P]
        TA = [_mm(X[p], jnp.concatenate([As[p], AV[p]], axis=1)) for p in P]
        lowV = [jnp.concatenate([zeros, Vs[p]], axis=1) for p in P]
        SY = [_mm(S[p], jnp.concatenate([TA[p], lowV[p]], axis=0)) for p in P]
        Rbar = [(Rs[p] + SY[p][:, :lanes]) * emid[p] for p in P]
        AU = [jnp.concatenate([TA[p][:, :lanes] * emid[p], TA[p][:, lanes:]], axis=1) for p in P]
        BKT = [jnp.transpose(x) for x in BKh]
        MN = [_mm(BKT[p], jnp.concatenate([AU[p], lowV[p]], axis=0)) for p in P]
        M = [eye * plast[p] + MN[p][:, :lanes] for p in P]
        H = [h_ref[p] for p in P]
        RH = [_mm(jnp.concatenate([Rbar[p], M[p]], axis=0), H[p]) for p in P]
        for p in P:
            h_ref[p] = RH[p][c2:] + MN[p][:, lanes:]
        Ys = [RH[p][:c2] + SY[p][:, lanes:] for p in P]
        y = [x[:chunk] + x[chunk:] for x in Ys]

        inv = 1.0 / RWKV_HEAD
        mean = seg_sums(y)
        yc = [y[p] - mean[p] * inv for p in P]
        var = seg_sums([x * x for x in yc])
        gn_g, gn_b = par(gg_ref), par(gb_ref)
        for p in P:
            yn = yc[p] * lax.rsqrt(var[p] * inv + GN_EPS) * gn_g[p] + gn_b[p]
            gate = g_ref[rows, cols[p]].astype(F32)
            o_ref[rows, cols[p]] = ((yn + bonus[p] * v[p]) * gate).astype(o_ref.dtype)
        return carry

    lax.fori_loop(0, r_ref.shape[0] // chunk, body, 0)


def wkv7(r, lw, k, v, a, g, k_k, k_a, r_k, gn_g, gn_b, batch, seq):
    n, d = r.shape
    pairs = min(WKV_PAIRS, d // V7X_LANES)
    width = pairs * V7X_LANES
    tb = _tile(seq, 256)
    seq_spec = pl.BlockSpec((tb, width), lambda b, p, t: (b * (seq // tb) + t, p))
    par_spec = pl.BlockSpec((1, width), lambda b, p, t: (0, p))
    row = lambda x: x.reshape(1, d).astype(F32)
    return pl.pallas_call(
        functools.partial(_wkv_kernel, chunk=WKV_CHUNK, pairs=pairs),
        grid=(batch, d // width, seq // tb),
        in_specs=[seq_spec] * 6 + [par_spec] * 5,
        out_specs=seq_spec,
        out_shape=jax.ShapeDtypeStruct((n, d), BF16),
        scratch_shapes=[pltpu.VMEM((pairs, V7X_LANES, V7X_LANES), F32)],
        compiler_params=_params("parallel", "parallel", "arbitrary"),
        name="wkv7",
    )(r, lw, k, v, a, g, row(k_k), row(k_a), row(r_k), row(gn_g), row(gn_b))


def _decay_epilogue(acc, w0):
    z = -(w0 + acc)
    softplus = jnp.maximum(z, 0.0) + jnp.log(1.0 + jnp.exp(-jnp.abs(z)))
    return -jnp.exp(-softplus - 0.5)


def rwkv_layer(h, batch, seq, norm_g, mix, w_r, w_k, w_v, w0, w1, w2, a0, a1, a2, g1, g2, k_k, k_a, r_k, gn_g,
               gn_b, w_o):
    d = h.shape[1]
    xs = rwkv_prep(h, norm_g, mix, seq)
    bf = lambda w: w.astype(BF16)

    def pad_cols(w):
        return jnp.pad(w, ((0, 0), (0, -w.shape[1] % V7X_LANES)))

    def pad_rows(w):
        return jnp.pad(w, ((0, -w.shape[0] % V7X_LANES), (0, 0)))

    r = matmul(xs, bf(w_r), x_sel=0, out_dtype=BF16, name="rwkv_r")
    k = matmul(xs, bf(w_k), x_sel=2, out_dtype=BF16, name="rwkv_k")
    v = matmul(xs, bf(w_v), x_sel=3, out_dtype=BF16, name="rwkv_v")
    tw = matmul(xs, bf(pad_cols(w1)), x_sel=1, out_dtype=BF16, epilogue=jnp.tanh, name="rwkv_w1")
    lw = matmul(tw, bf(pad_rows(w2)), out_dtype=F32, rows=(w0,), name="rwkv_w2",
                epilogue=_decay_epilogue)
    ta = matmul(xs, bf(pad_cols(a1)), x_sel=4, out_dtype=BF16, name="rwkv_a1")
    a = matmul(ta, bf(pad_rows(a2)), out_dtype=BF16, rows=(a0,), name="rwkv_a2",
               epilogue=lambda acc, b: jax.nn.sigmoid(b + acc))
    tg = matmul(xs, bf(pad_cols(g1)), x_sel=5, out_dtype=BF16, epilogue=jax.nn.sigmoid, name="rwkv_g1")
    g = matmul(tg, bf(pad_rows(g2)), out_dtype=BF16, name="rwkv_g2")
    y = wkv7(r, lw, k, v, a, g, k_k, k_a, r_k, gn_g, gn_b, batch, seq)
    return matmul(y, bf(w_o), out_dtype=F32, tiles=(h,), epilogue=lambda acc, res: res + acc, name="rwkv_o")


LOG2_E = 1.4426950408889634
ATTN_UNSHIFTED_MAX_LOG2 = 80.0
ATTN_UNROLL = 4


def _sub_head_queries(q):
    lane = lax.broadcasted_iota(jnp.int32, q.shape, 1)
    zero = jnp.zeros_like(q)
    return jnp.where(lane < DIFF_HEAD, q, zero), jnp.where(lane < DIFF_HEAD, zero, q)


def _causal_tile(tq):
    return lax.broadcasted_iota(jnp.int32, (tq, tq), 0) >= lax.broadcasted_iota(jnp.int32, (tq, tq), 1)


def _attn_online_kernel(lam_ref, q_ref, k_ref, v_ref, sg_ref, o_ref, *, tq, out_scale):
    qi = pl.program_id(2)
    qs = _sub_head_queries(q_ref[...])
    causal = _causal_tile(tq)

    def step(j, carry, masked):
        start = pl.multiple_of(j * tq, tq)
        kj = k_ref[pl.ds(start, tq), :]
        vj = v_ref[pl.ds(start, tq), :]
        out = []
        for sub in range(2):
            m, l, acc = carry[sub]
            s = lax.dot_general(qs[sub], kj, (((1,), (1,)), ((), ())), preferred_element_type=F32)
            if masked:
                s = jnp.where(causal, s, -1e30)
            m_new = jnp.maximum(m, jnp.max(s, axis=-1, keepdims=True))
            alpha = jnp.exp2(m - m_new)
            p = jnp.exp2(s - m_new)
            l = alpha * l + jnp.sum(p, axis=-1, keepdims=True)
            acc = alpha * acc + jnp.dot(p.astype(BF16), vj, preferred_element_type=F32)
            out.append((m_new, l, acc))
        return tuple(out)

    init = tuple((jnp.full((tq, 1), -1e30, F32), jnp.zeros((tq, 1), F32), jnp.zeros((tq, V7X_LANES), F32))
                 for _ in range(2))
    carry = lax.fori_loop(0, qi, lambda j, c: step(j, c, False), init)
    (_, l0, acc0), (_, l1, acc1) = step(qi, carry, True)
    o = acc0 / l0 - lam_ref[0] * (acc1 / l1)
    o_ref[...] = (_rms(o, sg_ref[...]) * out_scale).astype(o_ref.dtype)


def _attn_unshifted_kernel(lam_ref, q_ref, k_ref, v_ref, sg_ref, o_ref, acc_ref, *, tq, out_scale):
    qi = pl.program_id(2)
    qs = _sub_head_queries(q_ref[...])
    causal = _causal_tile(tq)
    ones = jnp.ones((tq, V7X_LANES), BF16)
    acc_ref[...] = jnp.zeros_like(acc_ref)

    def step(j, masked):
        start = pl.multiple_of(j * tq, tq)
        kj = k_ref[pl.ds(start, tq), :]
        vj = jnp.concatenate([v_ref[pl.ds(start, tq), :], ones], axis=1)
        for sub in range(2):
            s = lax.dot_general(qs[sub], kj, (((1,), (1,)), ((), ())), preferred_element_type=F32)
            if masked:
                s = jnp.where(causal, s, -1e30)
            acc_ref[sub] += jnp.dot(jnp.exp2(s).astype(BF16), vj, preferred_element_type=F32)

    def body(jj, c):
        for u in range(ATTN_UNROLL):
            step(jj * ATTN_UNROLL + u, False)
        return c

    n_full = qi // ATTN_UNROLL
    lax.fori_loop(0, n_full, body, 0)
    for u in range(ATTN_UNROLL - 1):
        @pl.when(n_full * ATTN_UNROLL + u < qi)
        def _():
            step(n_full * ATTN_UNROLL + u, False)
    step(qi, True)
    a0 = acc_ref[0]
    a1 = acc_ref[1]
    o = a0[:, :V7X_LANES] / a0[:, V7X_LANES:] - lam_ref[0] * (a1[:, :V7X_LANES] / a1[:, V7X_LANES:])
    o_ref[...] = (_rms(o, sg_ref[...]) * out_scale).astype(o_ref.dtype)


def diff_attention_core(q, k, v, lam, subln_g, batch, seq, lam_init, unshifted):
    n, d = q.shape
    tq = _tile(seq, 512)
    nq = seq // tq
    head_w = 2 * DIFF_HEAD
    if unshifted:
        body = _attn_unshifted_kernel
        scratch = [pltpu.VMEM((2, tq, 2 * head_w), F32)]
    else:
        body = _attn_online_kernel
        scratch = []
    return pl.pallas_call(
        functools.partial(body, tq=tq, out_scale=1.0 - lam_init),
        grid=(batch, d // head_w, nq),
        in_specs=[
            pl.BlockSpec(memory_space=pltpu.SMEM),
            pl.BlockSpec((tq, head_w), lambda b, h, i: (b * nq + i, h)),
            pl.BlockSpec((seq, head_w), lambda b, h, i: (b, h)),
            pl.BlockSpec((seq, head_w), lambda b, h, i: (b, h)),
            pl.BlockSpec((1, head_w), lambda b, h, i: (0, 0)),
        ],
        out_specs=pl.BlockSpec((tq, head_w), lambda b, h, i: (b * nq + i, h)),
        out_shape=jax.ShapeDtypeStruct((n, d), BF16),
        scratch_shapes=scratch,
        compiler_params=_params("parallel", "parallel", "arbitrary"),
        name="diff_attn_unshifted" if unshifted else "diff_attn_online",
    )(lam.reshape(1).astype(F32), q, k, v, subln_g.reshape(1, head_w).astype(F32))


def shared_kv(h, kv_norm_g, w_kv, k_norm_g):
    d = h.shape[1]
    hn = rmsnorm(h, kv_norm_g)
    kg = jnp.tile(k_norm_g, d // DIFF_HEAD)
    w = w_kv.astype(BF16)
    k = matmul(hn, w, w_cols=(0, d), out_dtype=BF16, rows=(kg,), name="kv_k", tn=512,
               epilogue=functools.partial(_seg_rms_epilogue, seg=DIFF_HEAD, scale=1.0))
    v = matmul(hn, w, w_cols=(d, d), out_dtype=BF16, name="kv_v")
    return k, v


def attn_layer(h, k, v, batch, seq, layer_idx, norm_g, w_q, q_norm_g, k_norm_g, lq1, lk1, lq2, lk2, subln_g, w_o):
    d = h.shape[1]
    hn = rmsnorm(h, norm_g)
    qg = jnp.tile(q_norm_g, d // DIFF_HEAD)
    q_scale = LOG2_E * DIFF_HEAD ** -0.5
    q = matmul(hn, w_q.astype(BF16), out_dtype=BF16, rows=(qg,), name="attn_q", tn=512,
               epilogue=functools.partial(_seg_rms_epilogue, seg=DIFF_HEAD, scale=q_scale))
    lam_init = 0.8 - 0.6 * math.exp(-0.3 * layer_idx)
    lam = jnp.exp(jnp.sum(lq1 * lk1)) - jnp.exp(jnp.sum(lq2 * lk2)) + lam_init
    bound = 1.01 * q_scale * DIFF_HEAD * jnp.max(jnp.abs(q_norm_g)) * jnp.max(jnp.abs(k_norm_g))
    core = lambda unshifted: functools.partial(diff_attention_core, batch=batch, seq=seq, lam_init=lam_init,
                                               unshifted=unshifted)
    o = lax.cond(bound <= ATTN_UNSHIFTED_MAX_LOG2, core(True), core(False), q, k, v, lam, subln_g)
    return matmul(o, w_o.astype(BF16), out_dtype=F32, tiles=(h,), epilogue=lambda acc, res: res + acc,
                  name="attn_o")


def _router_kernel(x_ref, g_ref, w_ref, b_ref, t_ref, logit_ref):
    t = _rms(x_ref[...], g_ref[...])
    t_ref[...] = t
    logit_ref[...] = jnp.dot(t, w_ref[...], preferred_element_type=F32, precision=lax.Precision.HIGHEST) + b_ref[...]


def moe_router(h, norm_g, rg_w, rg_b, re_w, re_b):
    n, d = h.shape
    tm = _tile(n, 256)
    n_log = rg_w.shape[1] + re_w.shape[1]
    pad = -n_log % V7X_LANES
    w = jnp.pad(jnp.concatenate([rg_w, re_w], axis=1), ((0, 0), (0, pad)))
    b = jnp.pad(jnp.concatenate([rg_b, re_b]), (0, pad)).reshape(1, -1)
    wl = w.shape[1]
    return pl.pallas_call(
        _router_kernel,
        grid=(n // tm,),
        in_specs=[pl.BlockSpec((tm, d), lambda i: (i, 0)), pl.BlockSpec((1, d), lambda i: (0, 0)),
                  pl.BlockSpec((d, wl), lambda i: (0, 0)), pl.BlockSpec((1, wl), lambda i: (0, 0))],
        out_specs=[pl.BlockSpec((tm, d), lambda i: (i, 0)), pl.BlockSpec((tm, wl), lambda i: (i, 0))],
        out_shape=[jax.ShapeDtypeStruct((n, d), F32), jax.ShapeDtypeStruct((n, wl), F32)],
        compiler_params=_params("parallel"),
        name="moe_router",
    )(h, norm_g.reshape(1, d), w, b)


def _gather_kernel(idx_ref, x_hbm, o_ref, sem, *, rows):
    def copy(r, src_row):
        return pltpu.make_async_copy(x_hbm.at[pl.ds(src_row, 1), :], o_ref.at[pl.ds(r, 1), :], sem)

    def issue(r, c):
        copy(r, idx_ref[0, 0, r]).start()
        return c

    def drain(r, c):
        copy(r, 0).wait()
        return c

    lax.fori_loop(0, rows, issue, 0)
    lax.fori_loop(0, rows, drain, 0)


def gather_rows(x, idx, rows_per_step=256):
    n_out = idx.shape[0]
    d = x.shape[1]
    bm = _tile(n_out, rows_per_step)
    return pl.pallas_call(
        functools.partial(_gather_kernel, rows=bm),
        grid=(n_out // bm,),
        in_specs=[pl.BlockSpec((1, 1, bm), lambda i: (i, 0, 0), memory_space=pltpu.SMEM),
                  pl.BlockSpec(memory_space=pl.ANY)],
        out_specs=pl.BlockSpec((bm, d), lambda i: (i, 0)),
        out_shape=jax.ShapeDtypeStruct((n_out, d), x.dtype),
        scratch_shapes=[pltpu.SemaphoreType.DMA(())],
        compiler_params=_params("arbitrary"),
        name="moe_gather",
    )(idx.reshape(n_out // bm, 1, bm), x)


def _expert_kernel(be_ref, nused_ref, x_ref, wg_ref, wu_ref, wd_ref, rw_ref, o_ref):
    i = pl.program_id(0)

    @pl.when(i < nused_ref[0])
    def _():
        x = x_ref[...].astype(BF16)
        hg = jnp.dot(x, wg_ref[...], preferred_element_type=F32)
        hu = jnp.dot(x, wu_ref[...], preferred_element_type=F32)
        act = (hg * jax.nn.sigmoid(hg) * hu).astype(BF16)
        o_ref[...] = jnp.dot(act, wd_ref[...], preferred_element_type=F32) * rw_ref[...]

    @pl.when(i >= nused_ref[0])
    def _():
        o_ref[...] = jnp.zeros_like(o_ref)


def moe_experts(xb, block_e, n_used, row_w, wg, wu, wd, bm):
    n_pad, d = xb.shape
    hid = wg.shape[2]
    return pl.pallas_call(
        _expert_kernel,
        grid_spec=pltpu.PrefetchScalarGridSpec(
            num_scalar_prefetch=2,
            grid=(n_pad // bm,),
            in_specs=[
                pl.BlockSpec((bm, d), lambda i, be, nu: (i, 0)),
                pl.BlockSpec((None, d, hid), lambda i, be, nu: (be[i], 0, 0)),
                pl.BlockSpec((None, d, hid), lambda i, be, nu: (be[i], 0, 0)),
                pl.BlockSpec((None, hid, d), lambda i, be, nu: (be[i], 0, 0)),
                pl.BlockSpec((bm, 1), lambda i, be, nu: (i, 0)),
            ],
            out_specs=pl.BlockSpec((bm, d), lambda i, be, nu: (i, 0)),
        ),
        out_shape=jax.ShapeDtypeStruct((n_pad, d), F32),
        compiler_params=_params("arbitrary"),
        name="moe_experts",
    )(block_e, n_used, xb, wg, wu, wd, row_w.reshape(n_pad, 1))


def _combine_kernel(idx_ref, yb_hbm, h_ref, o_ref, buf, sem, *, rows):
    def copy(r, s, src_row):
        return pltpu.make_async_copy(yb_hbm.at[pl.ds(src_row, 1), :], buf.at[s, pl.ds(r, 1), :], sem)

    def issue(r, c):
        for s in range(TOP_K):
            copy(r, s, idx_ref[0, s, r]).start()
        return c

    def drain(r, c):
        for s in range(TOP_K):
            copy(r, s, 0).wait()
        return c

    lax.fori_loop(0, rows, issue, 0)
    lax.fori_loop(0, rows, drain, 0)
    acc = h_ref[...]
    for s in range(TOP_K):
        acc = acc + buf[s]
    o_ref[...] = acc


def moe_combine(h, yb, pos, rows_per_step=256):
    n, d = h.shape
    tc = _tile(n, rows_per_step)
    idx = pos.reshape(n // tc, tc, TOP_K).transpose(0, 2, 1)
    return pl.pallas_call(
        functools.partial(_combine_kernel, rows=tc),
        grid=(n // tc,),
        in_specs=[pl.BlockSpec((1, TOP_K, tc), lambda i: (i, 0, 0), memory_space=pltpu.SMEM),
                  pl.BlockSpec(memory_space=pl.ANY),
                  pl.BlockSpec((tc, d), lambda i: (i, 0))],
        out_specs=pl.BlockSpec((tc, d), lambda i: (i, 0)),
        out_shape=jax.ShapeDtypeStruct((n, d), F32),
        scratch_shapes=[pltpu.VMEM((TOP_K, tc, d), F32), pltpu.SemaphoreType.DMA(())],
        compiler_params=_params("arbitrary"),
        name="moe_combine",
    )(idx, yb, h)


def moe_layer(h, norm_g, rg_w, rg_b, re_w, re_b, wg, wu, wd, bm=256):
    n, d = h.shape
    n_exp = wg.shape[0]
    epg = n_exp // N_GROUPS
    t, logits = moe_router(h, norm_g, rg_w, rg_b, re_w, re_b)
    g_logit = logits[:, :N_GROUPS]
    e_logit = logits[:, N_GROUPS:N_GROUPS + n_exp].reshape(n, N_GROUPS, epg)
    g_prob = jax.nn.softmax(g_logit, axis=-1)
    grp = jnp.argmax(g_logit, axis=-1).astype(jnp.int32)
    e_sel = jnp.take_along_axis(e_logit, grp[:, None, None], axis=1)[:, 0]
    top_v, top_i = lax.top_k(e_sel, TOP_K)
    gate = jnp.take_along_axis(g_prob, grp[:, None], axis=1) * jax.nn.softmax(top_v, axis=-1)
    eidx = grp[:, None] * epg + top_i.astype(jnp.int32)

    n_assign = n * TOP_K
    n_pad = n_assign + n_exp * bm
    flat_e = eidx.reshape(-1)
    order = jnp.argsort(flat_e).astype(jnp.int32)
    rank = jnp.argsort(order).astype(jnp.int32)
    counts = jnp.sum(flat_e[:, None] == jnp.arange(n_exp, dtype=jnp.int32)[None, :], axis=0, dtype=jnp.int32)
    padded = (counts + bm - 1) // bm * bm
    padded_end = jnp.cumsum(padded)
    padded_start = padded_end - padded
    start = jnp.cumsum(counts) - counts
    pos = (padded_start[flat_e] + rank - start[flat_e]).reshape(n, TOP_K)
    n_blocks = n_pad // bm
    block_start = jnp.arange(n_blocks, dtype=jnp.int32) * bm
    block_e = jnp.sum(block_start[:, None] >= padded_end[None, :], axis=1, dtype=jnp.int32)
    block_e = jnp.minimum(block_e, n_exp - 1)
    n_used = (padded_end[-1] // bm).astype(jnp.int32).reshape(1)
    slot_e = jnp.repeat(block_e, bm)
    off = jnp.arange(n_pad, dtype=jnp.int32) - padded_start[slot_e]
    valid = off < counts[slot_e]
    src = order[jnp.clip(start[slot_e] + off, 0, n_assign - 1)]
    buf_tok = jnp.where(valid, src // TOP_K, 0)
    buf_w = jnp.where(valid, gate.reshape(-1)[src], 0.0)

    xb = gather_rows(t, buf_tok, bm)
    yb = moe_experts(xb, block_e, n_used, buf_w, wg.astype(BF16), wu.astype(BF16), wd.astype(BF16), bm)
    return moe_combine(h, yb, pos)


def kernel(x, rwkv_norm_g, rwkv_mix, rwkv_w_r, rwkv_w_k, rwkv_w_v, rwkv_w0, rwkv_w1, rwkv_w2, rwkv_a0, rwkv_a1,
           rwkv_a2, rwkv_g1, rwkv_g2, rwkv_k_k, rwkv_k_a, rwkv_r_k, rwkv_gn_g, rwkv_gn_b, rwkv_w_o, kv_norm_g,
           w_kv, k_norm_g, attn_norm_g, attn_w_q, q_norm_g, lambda_q1, lambda_k1, lambda_q2, lambda_k2, subln_g,
           attn_w_o, moe_norm_g, router_group_w, router_group_b, router_expert_w, router_expert_b, expert_w_gate,
           expert_w_up, expert_w_down):
    batch, seq, d = x.shape
    depth = moe_norm_g.shape[0]
    n_a = rwkv_norm_g.shape[0]
    h = x.reshape(batch * seq, d)
    k_shared = v_shared = None
    for l in range(depth):
        if l < n_a:
            i = l
            h = rwkv_layer(h, batch, seq, rwkv_norm_g[i], rwkv_mix[i], rwkv_w_r[i], rwkv_w_k[i], rwkv_w_v[i],
                           rwkv_w0[i], rwkv_w1[i], rwkv_w2[i], rwkv_a0[i], rwkv_a1[i], rwkv_a2[i], rwkv_g1[i],
                           rwkv_g2[i], rwkv_k_k[i], rwkv_k_a[i], rwkv_r_k[i], rwkv_gn_g[i], rwkv_gn_b[i],
                           rwkv_w_o[i])
        else:
            j = l - n_a
            if j == 0:
                k_shared, v_shared = shared_kv(h, kv_norm_g, w_kv, k_norm_g)
            h = attn_layer(h, k_shared, v_shared, batch, seq, l, attn_norm_g[j], attn_w_q[j], q_norm_g[j], k_norm_g,
                           lambda_q1[j], lambda_k1[j], lambda_q2[j], lambda_k2[j], subln_g[j], attn_w_o[j])
        h = moe_layer(h, moe_norm_g[l], router_group_w[l], router_group_b[l], router_expert_w[l],
                      router_expert_b[l], expert_w_gate[l], expert_w_up[l], expert_w_down[l])
    return h.reshape(batch, seq, d)
```

```python
import functools
import math

import jax
import jax.numpy as jnp
from jax import lax
from jax.experimental import pallas as pl
from jax.experimental.pallas import tpu as pltpu

V7X_LANES = 128
V7X_SUBLANES = 8
V7X_VMEM_BYTES = 64 * 1024 * 1024
VMEM_LIMIT_BYTES = V7X_VMEM_BYTES - 8 * 1024 * 1024

RWKV_HEAD = 64
DIFF_HEAD = 64
N_GROUPS = 4
TOP_K = 2
GN_EPS = 64e-5
RMS_EPS = 1e-6
WKV_CHUNK = 64
WKV_PAIRS = 16

BF16 = jnp.bfloat16
F32 = jnp.float32


def _tile(n, pref):
    t = min(n, pref)
    assert n % t == 0, (n, pref)
    return t


def _params(*sem):
    return pltpu.CompilerParams(dimension_semantics=sem, vmem_limit_bytes=VMEM_LIMIT_BYTES)


def _mm(a, b):
    return jnp.dot(a.astype(BF16), b.astype(BF16), preferred_element_type=F32)


def _mm_nt(a, b):
    return lax.dot_general(a.astype(BF16), b.astype(BF16), (((1,), (1,)), ((), ())), preferred_element_type=F32)


def _split_dot(x, w_bf16):
    hi = x.astype(BF16)
    lo = (x - hi.astype(F32)).astype(BF16)
    return (jnp.dot(hi, w_bf16, preferred_element_type=F32) + jnp.dot(lo, w_bf16, preferred_element_type=F32))


def _seg_ones(n, seg):
    i = lax.broadcasted_iota(jnp.int32, (n, n), 0) // seg
    j = lax.broadcasted_iota(jnp.int32, (n, n), 1) // seg
    return jnp.where(i == j, 1.0, 0.0).astype(BF16)


def _rms(x, g):
    return x * lax.rsqrt(jnp.mean(x * x, axis=-1, keepdims=True) + RMS_EPS) * g


def _rmsnorm_kernel(x_ref, g_ref, o_ref):
    o_ref[...] = _rms(x_ref[...], g_ref[...]).astype(o_ref.dtype)


def rmsnorm(x, g, out_dtype=BF16):
    n, d = x.shape
    tm = _tile(n, 512)
    return pl.pallas_call(
        _rmsnorm_kernel,
        grid=(n // tm,),
        in_specs=[pl.BlockSpec((tm, d), lambda i: (i, 0)), pl.BlockSpec((1, d), lambda i: (0, 0))],
        out_specs=pl.BlockSpec((tm, d), lambda i: (i, 0)),
        out_shape=jax.ShapeDtypeStruct((n, d), out_dtype),
        compiler_params=_params("parallel"),
        name="rmsnorm",
    )(x, g.reshape(1, d))


def _rwkv_prep_kernel(x_ref, prev_ref, g_ref, mix_ref, o_ref, *, tm, seq):
    i = pl.program_id(0)
    g = g_ref[...]
    xn = _rms(x_ref[...], g)
    prev = _rms(prev_ref[...], g)[V7X_SUBLANES - 1:V7X_SUBLANES]
    prev = jnp.where((i * tm) % seq == 0, 0.0, prev)
    row = lax.broadcasted_iota(jnp.int32, xn.shape, 0)
    shifted = jnp.where(row == 0, prev, pltpu.roll(xn, 1, 0))
    xx = shifted - xn
    for s in range(o_ref.shape[0]):
        o_ref[s] = (xn + xx * mix_ref[s:s + 1, :]).astype(o_ref.dtype)


def rwkv_prep(h, g, mix, seq):
    n, d = h.shape
    n_mix = mix.shape[0]
    tm = _tile(seq, 256)
    per = tm // V7X_SUBLANES
    return pl.pallas_call(
        functools.partial(_rwkv_prep_kernel, tm=tm, seq=seq),
        grid=(n // tm,),
        in_specs=[
            pl.BlockSpec((tm, d), lambda i: (i, 0)),
            pl.BlockSpec((V7X_SUBLANES, d), lambda i: (jnp.maximum(i * per - 1, 0), 0)),
            pl.BlockSpec((1, d), lambda i: (0, 0)),
            pl.BlockSpec((n_mix, d), lambda i: (0, 0)),
        ],
        out_specs=pl.BlockSpec((n_mix, tm, d), lambda i: (0, i, 0)),
        out_shape=jax.ShapeDtypeStruct((n_mix, n, d), BF16),
        compiler_params=_params("parallel"),
        name="rwkv_prep",
    )(h, h, g.reshape(1, d), mix)


def _matmul_kernel(x_ref, w_ref, *rest, epilogue, n_extra):
    extra = [r[...] for r in rest[:n_extra]]
    o_ref = rest[n_extra]
    acc = jnp.dot(x_ref[...], w_ref[...], preferred_element_type=F32)
    o_ref[...] = epilogue(acc, *extra).astype(o_ref.dtype)


def matmul(x, w, *, out_dtype, epilogue=None, rows=(), tiles=(), x_sel=None, w_cols=None, tm=1024, tn=1024,
           name="matmul"):
    if x_sel is None:
        m, k = x.shape
        x_spec = lambda tm_: pl.BlockSpec((tm_, k), lambda i, j: (i, 0))
    else:
        _, m, k = x.shape
        x_spec = lambda tm_: pl.BlockSpec((None, tm_, k), lambda i, j: (x_sel, i, 0))
    col0, n_out = (0, w.shape[1]) if w_cols is None else w_cols
    tm = _tile(m, tm)
    tn = _tile(n_out, tn)
    assert col0 % tn == 0
    j0 = col0 // tn
    if epilogue is None:
        epilogue = lambda acc: acc
    in_specs = [x_spec(tm), pl.BlockSpec((k, tn), lambda i, j: (0, j + j0))]
    in_specs += [pl.BlockSpec((1, tn), lambda i, j: (0, j)) for _ in rows]
    in_specs += [pl.BlockSpec((tm, tn), lambda i, j: (i, j)) for _ in tiles]
    return pl.pallas_call(
        functools.partial(_matmul_kernel, epilogue=epilogue, n_extra=len(rows) + len(tiles)),
        grid=(m // tm, n_out // tn),
        in_specs=in_specs,
        out_specs=pl.BlockSpec((tm, tn), lambda i, j: (i, j)),
        out_shape=jax.ShapeDtypeStruct((m, n_out), out_dtype),
        compiler_params=_params("parallel", "arbitrary"),
        name=name,
    )(x, w, *[r.reshape(1, n_out).astype(F32) for r in rows], *tiles)


def _seg_rms_epilogue(acc, g, *, seg, scale):
    ones = _seg_ones(V7X_LANES, seg)
    outs = []
    for s in range(acc.shape[1] // V7X_LANES):
        a = acc[:, s * V7X_LANES:(s + 1) * V7X_LANES]
        ss = _split_dot(a * a, ones)
        outs.append(a * lax.rsqrt(ss * (1.0 / seg) + RMS_EPS))
    return jnp.concatenate(outs, axis=1) * (g * scale)


def _wkv_kernel(r_ref, lw_ref, k_ref, v_ref, a_ref, g_ref, kk_ref, ka_ref, rk_ref, gg_ref, gb_ref, o_ref, h_ref,
                *, chunk, pairs):
    c2 = 2 * chunk
    half = V7X_LANES // 2
    assert c2 == V7X_LANES, "two stacked heads of one chunk must fill one 128-row tile"

    @pl.when(pl.program_id(2) == 0)
    def _():
        h_ref[...] = jnp.zeros_like(h_ref)

    lane = lax.broadcasted_iota(jnp.int32, (chunk, V7X_LANES), 1)
    head0 = lane < half
    ri = lax.broadcasted_iota(jnp.int32, (c2, c2), 0)
    ci = lax.broadcasted_iota(jnp.int32, (c2, c2), 1)
    same = (ri // chunk) == (ci // chunk)
    strict = jnp.logical_and(same, (ci % chunk) < (ri % chunk))
    incl = jnp.logical_and(same, (ci % chunk) <= (ri % chunk))
    eye = jnp.where(ri == ci, 1.0, 0.0)
    li = lax.broadcasted_iota(jnp.int32, (chunk, chunk), 0)
    lj = lax.broadcasted_iota(jnp.int32, (chunk, chunk), 1)
    ltri = jnp.where(li >= lj, 1.0, 0.0).astype(BF16)
    seg = _seg_ones(V7X_LANES, RWKV_HEAD)
    n_double = int(math.log2(chunk)) - 1

    def stack(x):
        return jnp.concatenate([jnp.where(head0, x, 0.0), jnp.where(head0, 0.0, x)], axis=0)

    def hi_lo(x, axis):
        hi = x.astype(BF16)
        return jnp.concatenate([hi, (x - hi.astype(F32)).astype(BF16)], axis=axis)

    def seg_sums(xs):
        prods = [jnp.dot(hi_lo(x, 0), seg, preferred_element_type=F32) for x in xs]
        return [o[:chunk] + o[chunk:] for o in prods]

    lanes = V7X_LANES
    P = range(pairs)
    cols = [slice(p * lanes, (p + 1) * lanes) for p in P]
    zeros = jnp.zeros((c2, lanes), F32)

    def body(c, carry):
        rows = pl.ds(pl.multiple_of(c * chunk, chunk), chunk)
        ld = lambda ref: [ref[rows, cs].astype(F32) for cs in cols]
        par = lambda ref: [ref[:, cs] for cs in cols]
        r, lw, k, v, a = ld(r_ref), ld(lw_ref), ld(k_ref), ld(v_ref), ld(a_ref)
        k_k, k_a, r_k = par(kk_ref), par(ka_ref), par(rk_ref)

        kk = [k[p] * k_k[p] for p in P]
        ss = seg_sums([x * x for x in kk])
        kk = [kk[p] * lax.rsqrt(jnp.maximum(ss[p], 1e-24)) for p in P]
        k = [k[p] * (1.0 + (a[p] - 1.0) * k_a[p]) for p in P]
        b = [kk[p] * a[p] for p in P]
        bonus = seg_sums([r[p] * k[p] * r_k[p] for p in P])

        cc = [jnp.dot(ltri, hi_lo(lw[p], 1), preferred_element_type=F32) for p in P]
        cum = [x[:, :lanes] + x[:, lanes:] for x in cc]
        mid = [x[chunk // 2 - 1:chunk // 2] for x in cum]
        last = [x[chunk - 1:chunk] for x in cum]
        dec_mid = [jnp.exp(mid[p] - cum[p]) for p in P]
        dec_last = [jnp.exp(last[p] - cum[p]) for p in P]
        emid = [jnp.exp(x) for x in mid]
        plast = [jnp.exp(x) for x in last]
        As = [stack(-kk[p] * jnp.exp(cum[p] - lw[p] - mid[p])) for p in P]
        Rs = [stack(r[p] * jnp.exp(cum[p] - mid[p])) for p in P]
        Vs = [stack(v[p]) for p in P]
        AR = [jnp.concatenate([As[p], Rs[p]], axis=0) for p in P]
        BK = [jnp.concatenate([stack(b[p] * dec_mid[p]), stack(k[p] * dec_mid[p])], axis=0) for p in P]
        BKh = [jnp.concatenate([stack(b[p] * dec_last[p]), stack(k[p] * dec_last[p])], axis=0) for p in P]

        G = [_mm_nt(AR[p], BK[p]) for p in P]
        Aab = [jnp.where(strict, g[:c2, :c2], 0.0) for g in G]
        Aak = [jnp.where(strict, g[:c2, c2:], 0.0) for g in G]
        S = [jnp.concatenate([jnp.where(incl, g[c2:, :c2], 0.0), jnp.where(incl, g[c2:, c2:], 0.0)], axis=1)
             for g in G]

        X = [eye + x for x in Aab]
        Q = [_mm(x, x) for x in Aab]
        for _ in range(n_double - 1):
            QX = [_mm(Q[p], jnp.concatenate([Q[p], X[p]], axis=1)) for p in P]
            X = [X[p] + QX[p][:, lanes:] for p in P]
            Q = [x[:, :lanes] for x in QX]
        X = [X[p] + _mm(Q[p], X[p]) for p in P]

        AV = [_mm(Aak[p], Vs[p]) for p in P]
        TA = [_mm(X[p], jnp.concatenate([As[p], AV[p]], axis=1)) for p in P]
        lowV = [jnp.concatenate([zeros, Vs[p]], axis=1) for p in P]
        SY = [_mm(S[p], jnp.concatenate([TA[p], lowV[p]], axis=0)) for p in P]
        Rbar = [(Rs[p] + SY[p][:, :lanes]) * emid[p] for p in P]
        AU = [jnp.concatenate([TA[p][:, :lanes] * emid[p], TA[p][:, lanes:]], axis=1) for p in P]
        BKT = [jnp.transpose(x) for x in BKh]
        MN = [_mm(BKT[p], jnp.concatenate([AU[p], lowV[p]], axis=0)) for p in P]
        M = [eye * plast[p] + MN[p][:, :lanes] for p in P]
        H = [h_ref[p] for p in P]
        RH = [_mm(jnp.concatenate([Rbar[p], M[p]], axis=0), H[p]) for p in P]
        for p in P:
            h_ref[p] = RH[p][c2:] + MN[p][:, lanes:]
        Ys = [RH[p][:c2] + SY[p][:, lanes:] for p in P]
        y = [x[:chunk] + x[chunk:] for x in Ys]

        inv = 1.0 / RWKV_HEAD
        mean = seg_sums(y)
        yc = [y[p] - mean[p] * inv for p in P]
        var = seg_sums([x * x for x in yc])
        gn_g, gn_b = par(gg_ref), par(gb_ref)
        for p in P:
            yn = yc[p] * lax.rsqrt(var[p] * inv + GN_EPS) * gn_g[p] + gn_b[p]
            gate = g_ref[rows, cols[p]].astype(F32)
            o_ref[rows, cols[p]] = ((yn + bonus[p] * v[p]) * gate).astype(o_ref.dtype)
        return carry

    lax.fori_loop(0, r_ref.shape[0] // chunk, body, 0)


def wkv7(r, lw, k, v, a, g, k_k, k_a, r_k, gn_g, gn_b, batch, seq):
    n, d = r.shape
    pairs = min(WKV_PAIRS, d // V7X_LANES)
    width = pairs * V7X_LANES
    tb = _tile(seq, 256)
    seq_spec = pl.BlockSpec((tb, width), lambda b, p, t: (b * (seq // tb) + t, p))
    par_spec = pl.BlockSpec((1, width), lambda b, p, t: (0, p))
    row = lambda x: x.reshape(1, d).astype(F32)
    return pl.pallas_call(
        functools.partial(_wkv_kernel, chunk=WKV_CHUNK, pairs=pairs),
        grid=(batch, d // width, seq // tb),
        in_specs=[seq_spec] * 6 + [par_spec] * 5,
        out_specs=seq_spec,
        out_shape=jax.ShapeDtypeStruct((n, d), BF16),
        scratch_shapes=[pltpu.VMEM((pairs, V7X_LANES, V7X_LANES), F32)],
        compiler_params=_params("parallel", "parallel", "arbitrary"),
        name="wkv7",
    )(r, lw, k, v, a, g, row(k_k), row(k_a), row(r_k), row(gn_g), row(gn_b))


def _decay_epilogue(acc, w0):
    z = -(w0 + acc)
    softplus = jnp.maximum(z, 0.0) + jnp.log(1.0 + jnp.exp(-jnp.abs(z)))
    return -jnp.exp(-softplus - 0.5)


def rwkv_layer(h, batch, seq, norm_g, mix, w_r, w_k, w_v, w0, w1, w2, a0, a1, a2, g1, g2, k_k, k_a, r_k, gn_g,
               gn_b, w_o):
    d = h.shape[1]
    xs = rwkv_prep(h, norm_g, mix, seq)
    bf = lambda w: w.astype(BF16)

    def pad_cols(w):
        return jnp.pad(w, ((0, 0), (0, -w.shape[1] % V7X_LANES)))

    def pad_rows(w):
        return jnp.pad(w, ((0, -w.shape[0] % V7X_LANES), (0, 0)))

    r = matmul(xs, bf(w_r), x_sel=0, out_dtype=BF16, name="rwkv_r")
    k = matmul(xs, bf(w_k), x_sel=2, out_dtype=BF16, name="rwkv_k")
    v = matmul(xs, bf(w_v), x_sel=3, out_dtype=BF16, name="rwkv_v")
    tw = matmul(xs, bf(pad_cols(w1)), x_sel=1, out_dtype=BF16, epilogue=jnp.tanh, name="rwkv_w1")
    lw = matmul(tw, bf(pad_rows(w2)), out_dtype=F32, rows=(w0,), name="rwkv_w2",
                epilogue=_decay_epilogue)
    ta = matmul(xs, bf(pad_cols(a1)), x_sel=4, out_dtype=BF16, name="rwkv_a1")
    a = matmul(ta, bf(pad_rows(a2)), out_dtype=BF16, rows=(a0,), name="rwkv_a2",
               epilogue=lambda acc, b: jax.nn.sigmoid(b + acc))
    tg = matmul(xs, bf(pad_cols(g1)), x_sel=5, out_dtype=BF16, epilogue=jax.nn.sigmoid, name="rwkv_g1")
    g = matmul(tg, bf(pad_rows(g2)), out_dtype=BF16, name="rwkv_g2")
    y = wkv7(r, lw, k, v, a, g, k_k, k_a, r_k, gn_g, gn_b, batch, seq)
    return matmul(y, bf(w_o), out_dtype=F32, tiles=(h,), epilogue=lambda acc, res: res + acc, name="rwkv_o")


LOG2_E = 1.4426950408889634
ATTN_UNSHIFTED_MAX_LOG2 = 80.0
ATTN_UNROLL = 4
ATTN_LAG = 1
ATTN_SUM_ROWS = 16


def _sub_head_queries(q):
    lane = lax.broadcasted_iota(jnp.int32, q.shape, 1)
    zero = jnp.zeros_like(q)
    return jnp.where(lane < DIFF_HEAD, q, zero), jnp.where(lane < DIFF_HEAD, zero, q)


def _causal_tile(tq):
    return lax.broadcasted_iota(jnp.int32, (tq, tq), 0) >= lax.broadcasted_iota(jnp.int32, (tq, tq), 1)


def _attn_online_kernel(lam_ref, q_ref, k_ref, v_ref, sg_ref, o_ref, *, tq, out_scale):
    qi = pl.program_id(2)
    qs = _sub_head_queries(q_ref[...])
    causal = _causal_tile(tq)

    def step(j, carry, masked):
        start = pl.multiple_of(j * tq, tq)
        kj = k_ref[pl.ds(start, tq), :]
        vj = v_ref[pl.ds(start, tq), :]
        out = []
        for sub in range(2):
            m, l, acc = carry[sub]
            s = lax.dot_general(qs[sub], kj, (((1,), (1,)), ((), ())), preferred_element_type=F32)
            if masked:
                s = jnp.where(causal, s, -1e30)
            m_new = jnp.maximum(m, jnp.max(s, axis=-1, keepdims=True))
            alpha = jnp.exp2(m - m_new)
            p = jnp.exp2(s - m_new)
            l = alpha * l + jnp.sum(p, axis=-1, keepdims=True)
            acc = alpha * acc + jnp.dot(p.astype(BF16), vj, preferred_element_type=F32)
            out.append((m_new, l, acc))
        return tuple(out)

    init = tuple((jnp.full((tq, 1), -1e30, F32), jnp.zeros((tq, 1), F32), jnp.zeros((tq, V7X_LANES), F32))
                 for _ in range(2))
    carry = lax.fori_loop(0, qi, lambda j, c: step(j, c, False), init)
    (_, l0, acc0), (_, l1, acc1) = step(qi, carry, True)
    o = acc0 / l0 - lam_ref[0] * (acc1 / l1)
    o_ref[...] = (_rms(o, sg_ref[...]) * out_scale).astype(o_ref.dtype)


def _attn_unshifted_kernel(lam_ref, q_ref, k_ref, vt_ref, sg_ref, o_ref, acc_ref, *, tq, out_scale):
    qi = pl.program_id(2)
    qs = _sub_head_queries(q_ref[...])
    visible = lax.broadcasted_iota(jnp.int32, (tq, tq), 1) >= lax.broadcasted_iota(jnp.int32, (tq, tq), 0)
    ones = jnp.ones((ATTN_SUM_ROWS, tq), BF16)
    acc_ref[...] = jnp.zeros_like(acc_ref)

    def scores(j, sub, masked):
        kj = k_ref[pl.ds(pl.multiple_of(j * tq, tq), tq), :]
        st = lax.dot_general(kj, qs[sub], (((1,), (1,)), ((), ())), preferred_element_type=F32)
        return jnp.where(visible, st, -1e30) if masked else st

    def accumulate(j, sub, st):
        vt = jnp.concatenate([vt_ref[j], ones], axis=0)
        acc_ref[sub] += jnp.dot(vt, jnp.exp2(st).astype(BF16), preferred_element_type=F32)

    def steps(tiles):
        work = [(j, sub, masked) for j, masked in tiles for sub in range(2)]
        pending = []
        for j, sub, masked in work:
            pending.append((j, sub, scores(j, sub, masked)))
            if len(pending) > ATTN_LAG:
                accumulate(*pending.pop(0))
        for item in pending:
            accumulate(*item)

    def body(jj, c):
        steps([(jj * ATTN_UNROLL + u, False) for u in range(ATTN_UNROLL)])
        return c

    n_full = qi // ATTN_UNROLL
    lax.fori_loop(0, n_full, body, 0)
    first = n_full * ATTN_UNROLL
    for rem in range(ATTN_UNROLL):
        @pl.when(qi - first == rem)
        def _():
            steps([(first + u, False) for u in range(rem)] + [(qi, True)])
    a0 = acc_ref[0]
    a1 = acc_ref[1]
    hw = 2 * DIFF_HEAD
    ot = a0[:hw] / a0[hw:hw + 1] - lam_ref[0] * (a1[:hw] / a1[hw:hw + 1])
    ot = ot * lax.rsqrt(jnp.mean(ot * ot, axis=0, keepdims=True) + RMS_EPS)
    o_ref[...] = (jnp.transpose(ot) * (sg_ref[...] * out_scale)).astype(o_ref.dtype)


def diff_attention_core(q, k, v, lam, subln_g, batch, seq, lam_init, unshifted):
    n, d = q.shape
    tq = _tile(seq, 512)
    nq = seq // tq
    head_w = 2 * DIFF_HEAD
    q_spec = pl.BlockSpec((tq, head_w), lambda b, h, i: (b * nq + i, h))
    kv_spec = pl.BlockSpec((seq, head_w), lambda b, h, i: (b, h))
    if unshifted:
        body = _attn_unshifted_kernel
        scratch = [pltpu.VMEM((2, head_w + ATTN_SUM_ROWS, tq), F32)]
        v = jnp.transpose(v.reshape(n // tq, tq, d), (0, 2, 1))
        v_spec = pl.BlockSpec((nq, head_w, tq), lambda b, h, i: (b, h, 0))
    else:
        body = _attn_online_kernel
        scratch = []
        v_spec = kv_spec
    return pl.pallas_call(
        functools.partial(body, tq=tq, out_scale=1.0 - lam_init),
        grid=(batch, d // head_w, nq),
        in_specs=[pl.BlockSpec(memory_space=pltpu.SMEM), q_spec, kv_spec, v_spec,
                  pl.BlockSpec((1, head_w), lambda b, h, i: (0, 0))],
        out_specs=q_spec,
        out_shape=jax.ShapeDtypeStruct((n, d), BF16),
        scratch_shapes=scratch,
        compiler_params=_params("parallel", "parallel", "arbitrary"),
        name="diff_attn_unshifted" if unshifted else "diff_attn_online",
    )(lam.reshape(1).astype(F32), q, k, v, subln_g.reshape(1, head_w).astype(F32))


def shared_kv(h, kv_norm_g, w_kv, k_norm_g):
    d = h.shape[1]
    hn = rmsnorm(h, kv_norm_g)
    kg = jnp.tile(k_norm_g, d // DIFF_HEAD)
    w = w_kv.astype(BF16)
    k = matmul(hn, w, w_cols=(0, d), out_dtype=BF16, rows=(kg,), name="kv_k", tn=512,
               epilogue=functools.partial(_seg_rms_epilogue, seg=DIFF_HEAD, scale=1.0))
    v = matmul(hn, w, w_cols=(d, d), out_dtype=BF16, name="kv_v")
    return k, v


def attn_layer(h, k, v, batch, seq, layer_idx, norm_g, w_q, q_norm_g, k_norm_g, lq1, lk1, lq2, lk2, subln_g, w_o):
    d = h.shape[1]
    hn = rmsnorm(h, norm_g)
    qg = jnp.tile(q_norm_g, d // DIFF_HEAD)
    q_scale = LOG2_E * DIFF_HEAD ** -0.5
    q = matmul(hn, w_q.astype(BF16), out_dtype=BF16, rows=(qg,), name="attn_q", tn=512,
               epilogue=functools.partial(_seg_rms_epilogue, seg=DIFF_HEAD, scale=q_scale))
    lam_init = 0.8 - 0.6 * math.exp(-0.3 * layer_idx)
    lam = jnp.exp(jnp.sum(lq1 * lk1)) - jnp.exp(jnp.sum(lq2 * lk2)) + lam_init
    bound = 1.01 * q_scale * DIFF_HEAD * jnp.max(jnp.abs(q_norm_g)) * jnp.max(jnp.abs(k_norm_g))
    core = lambda unshifted: functools.partial(diff_attention_core, batch=batch, seq=seq, lam_init=lam_init,
                                               unshifted=unshifted)
    o = lax.cond(bound <= ATTN_UNSHIFTED_MAX_LOG2, core(True), core(False), q, k, v, lam, subln_g)
    return matmul(o, w_o.astype(BF16), out_dtype=F32, tiles=(h,), epilogue=lambda acc, res: res + acc,
                  name="attn_o")


PACK_ROWS = V7X_SUBLANES


def _pack_store(x, o_ref):
    m = x.shape[0]
    assert x.shape[1] == 2 * PACK_ROWS * V7X_LANES
    word = lambda j: pltpu.bitcast(x[:, j * V7X_LANES:(j + 1) * V7X_LANES].astype(BF16).astype(F32), jnp.uint32)
    for s in range(PACK_ROWS):
        o_ref[pl.ds(s, m, stride=PACK_ROWS), :] = word(s) | (word(s + PACK_ROWS) >> 16)


def _unpack_load(p_ref, m):
    words = [p_ref[pl.ds(s, m, stride=PACK_ROWS), :] for s in range(PACK_ROWS)]
    hi = [pltpu.bitcast(w & jnp.uint32(0xFFFF0000), F32) for w in words]
    lo = [pltpu.bitcast(w << 16, F32) for w in words]
    return jnp.concatenate(hi + lo, axis=1)


def _router_kernel(x_ref, g_ref, w_ref, b_ref, t_ref, logit_ref):
    t = _rms(x_ref[...], g_ref[...])
    _pack_store(t, t_ref)
    logit_ref[...] = jnp.dot(t, w_ref[...], preferred_element_type=F32, precision=lax.Precision.HIGHEST) + b_ref[...]


def moe_router(h, norm_g, rg_w, rg_b, re_w, re_b):
    n, d = h.shape
    tm = _tile(n, 256)
    n_log = rg_w.shape[1] + re_w.shape[1]
    pad = -n_log % V7X_LANES
    w = jnp.pad(jnp.concatenate([rg_w, re_w], axis=1), ((0, 0), (0, pad)))
    b = jnp.pad(jnp.concatenate([rg_b, re_b]), (0, pad)).reshape(1, -1)
    wl = w.shape[1]
    return pl.pallas_call(
        _router_kernel,
        grid=(n // tm,),
        in_specs=[pl.BlockSpec((tm, d), lambda i: (i, 0)), pl.BlockSpec((1, d), lambda i: (0, 0)),
                  pl.BlockSpec((d, wl), lambda i: (0, 0)), pl.BlockSpec((1, wl), lambda i: (0, 0))],
        out_specs=[pl.BlockSpec((tm * PACK_ROWS, V7X_LANES), lambda i: (i, 0)),
                   pl.BlockSpec((tm, wl), lambda i: (i, 0))],
        out_shape=[jax.ShapeDtypeStruct((n * PACK_ROWS, V7X_LANES), jnp.uint32),
                   jax.ShapeDtypeStruct((n, wl), F32)],
        compiler_params=_params("parallel"),
        name="moe_router",
    )(h, norm_g.reshape(1, d), w, b)


def _token_copy(src_hbm, token, dst, slot, sem):
    src = src_hbm.at[pl.ds(pl.multiple_of(token * PACK_ROWS, PACK_ROWS), PACK_ROWS), :]
    return pltpu.make_async_copy(src, dst.at[pl.ds(pl.multiple_of(slot * PACK_ROWS, PACK_ROWS), PACK_ROWS), :], sem)


DMA_LOOP_UNROLL = 8


def _gather_kernel(idx_ref, x_hbm, o_ref, sem, *, rows):
    def issue(r, c):
        _token_copy(x_hbm, idx_ref[0, 0, r], o_ref, r, sem).start()
        return c

    def drain(r, c):
        _token_copy(x_hbm, 0, o_ref, r, sem).wait()
        return c

    lax.fori_loop(0, rows, issue, 0, unroll=DMA_LOOP_UNROLL)
    lax.fori_loop(0, rows, drain, 0, unroll=DMA_LOOP_UNROLL)


def gather_rows(x, idx, rows_per_step=256):
    n_out = idx.shape[0]
    bm = _tile(n_out, rows_per_step)
    return pl.pallas_call(
        functools.partial(_gather_kernel, rows=bm),
        grid=(n_out // bm,),
        in_specs=[pl.BlockSpec((1, 1, bm), lambda i: (i, 0, 0), memory_space=pltpu.SMEM),
                  pl.BlockSpec(memory_space=pl.ANY)],
        out_specs=pl.BlockSpec((bm * PACK_ROWS, V7X_LANES), lambda i: (i, 0)),
        out_shape=jax.ShapeDtypeStruct((n_out * PACK_ROWS, V7X_LANES), x.dtype),
        scratch_shapes=[pltpu.SemaphoreType.DMA(())],
        compiler_params=_params("arbitrary"),
        name="moe_gather",
    )(idx.reshape(n_out // bm, 1, bm), x)


def _expert_kernel(be_ref, nused_ref, x_ref, wg_ref, wu_ref, wd_ref, rw_ref, o_ref, *, bm):
    i = pl.program_id(0)

    @pl.when(i < nused_ref[0])
    def _():
        x = _unpack_load(x_ref, bm).astype(BF16)
        hg = jnp.dot(x, wg_ref[...], preferred_element_type=F32)
        hu = jnp.dot(x, wu_ref[...], preferred_element_type=F32)
        act = (hg * jax.nn.sigmoid(hg) * hu).astype(BF16)
        _pack_store(jnp.dot(act, wd_ref[...], preferred_element_type=F32) * rw_ref[...], o_ref)

    @pl.when(i >= nused_ref[0])
    def _():
        o_ref[...] = jnp.zeros_like(o_ref)


def moe_experts(xb, block_e, n_used, row_w, wg, wu, wd, bm):
    n_pad = row_w.shape[0]
    d, hid = wg.shape[1], wg.shape[2]
    packed = pl.BlockSpec((bm * PACK_ROWS, V7X_LANES), lambda i, be, nu: (i, 0))
    return pl.pallas_call(
        functools.partial(_expert_kernel, bm=bm),
        grid_spec=pltpu.PrefetchScalarGridSpec(
            num_scalar_prefetch=2,
            grid=(n_pad // bm,),
            in_specs=[
                packed,
                pl.BlockSpec((None, d, hid), lambda i, be, nu: (be[i], 0, 0)),
                pl.BlockSpec((None, d, hid), lambda i, be, nu: (be[i], 0, 0)),
                pl.BlockSpec((None, hid, d), lambda i, be, nu: (be[i], 0, 0)),
                pl.BlockSpec((bm, 1), lambda i, be, nu: (i, 0)),
            ],
            out_specs=packed,
        ),
        out_shape=jax.ShapeDtypeStruct(xb.shape, xb.dtype),
        compiler_params=_params("arbitrary"),
        name="moe_experts",
    )(block_e, n_used, xb, wg, wu, wd, row_w.reshape(n_pad, 1))


def _combine_kernel(idx_ref, yb_hbm, h_ref, o_ref, buf, sem, *, rows):
    def issue(r, c):
        for s in range(TOP_K):
            _token_copy(yb_hbm, idx_ref[0, s, r], buf.at[s], r, sem).start()
        return c

    def drain(r, c):
        for s in range(TOP_K):
            _token_copy(yb_hbm, 0, buf.at[s], r, sem).wait()
        return c

    lax.fori_loop(0, rows, issue, 0, unroll=DMA_LOOP_UNROLL)
    lax.fori_loop(0, rows, drain, 0, unroll=DMA_LOOP_UNROLL)
    acc = h_ref[...]
    for s in range(TOP_K):
        acc = acc + _unpack_load(buf.at[s], rows)
    o_ref[...] = acc


def moe_combine(h, yb, pos, rows_per_step=256):
    n, d = h.shape
    tc = _tile(n, rows_per_step)
    idx = pos.reshape(n // tc, tc, TOP_K).transpose(0, 2, 1)
    return pl.pallas_call(
        functools.partial(_combine_kernel, rows=tc),
        grid=(n // tc,),
        in_specs=[pl.BlockSpec((1, TOP_K, tc), lambda i: (i, 0, 0), memory_space=pltpu.SMEM),
                  pl.BlockSpec(memory_space=pl.ANY),
                  pl.BlockSpec((tc, d), lambda i: (i, 0))],
        out_specs=pl.BlockSpec((tc, d), lambda i: (i, 0)),
        out_shape=jax.ShapeDtypeStruct((n, d), F32),
        scratch_shapes=[pltpu.VMEM((TOP_K, tc * PACK_ROWS, V7X_LANES), jnp.uint32), pltpu.SemaphoreType.DMA(())],
        compiler_params=_params("arbitrary"),
        name="moe_combine",
    )(idx, yb, h)


def moe_layer(h, norm_g, rg_w, rg_b, re_w, re_b, wg, wu, wd, bm=256):
    n, d = h.shape
    n_exp = wg.shape[0]
    epg = n_exp // N_GROUPS
    t, logits = moe_router(h, norm_g, rg_w, rg_b, re_w, re_b)
    g_logit = logits[:, :N_GROUPS]
    e_logit = logits[:, N_GROUPS:N_GROUPS + n_exp].reshape(n, N_GROUPS, epg)
    g_prob = jax.nn.softmax(g_logit, axis=-1)
    grp = jnp.argmax(g_logit, axis=-1).astype(jnp.int32)
    e_sel = jnp.take_along_axis(e_logit, grp[:, None, None], axis=1)[:, 0]
    top_v, top_i = lax.top_k(e_sel, TOP_K)
    gate = jnp.take_along_axis(g_prob, grp[:, None], axis=1) * jax.nn.softmax(top_v, axis=-1)
    eidx = grp[:, None] * epg + top_i.astype(jnp.int32)

    n_assign = n * TOP_K
    n_pad = n_assign + n_exp * bm
    flat_e = eidx.reshape(-1)
    order = jnp.argsort(flat_e).astype(jnp.int32)
    rank = jnp.argsort(order).astype(jnp.int32)
    counts = jnp.sum(flat_e[None, :] == jnp.arange(n_exp, dtype=jnp.int32)[:, None], axis=1, dtype=jnp.int32)
    padded = (counts + bm - 1) // bm * bm
    padded_end = jnp.cumsum(padded)
    padded_start = padded_end - padded
    start = jnp.cumsum(counts) - counts
    pos = (padded_start[flat_e] + rank - start[flat_e]).reshape(n, TOP_K)
    n_blocks = n_pad // bm
    block_start = jnp.arange(n_blocks, dtype=jnp.int32) * bm
    block_e = jnp.sum(block_start[:, None] >= padded_end[None, :], axis=1, dtype=jnp.int32)
    block_e = jnp.minimum(block_e, n_exp - 1)
    n_used = (padded_end[-1] // bm).astype(jnp.int32).reshape(1)
    slot_e = jnp.repeat(block_e, bm)
    off = jnp.arange(n_pad, dtype=jnp.int32) - padded_start[slot_e]
    valid = off < counts[slot_e]
    src = order[jnp.clip(start[slot_e] + off, 0, n_assign - 1)]
    buf_tok = jnp.where(valid, src // TOP_K, 0)
    buf_w = jnp.where(valid, gate.reshape(-1)[src], 0.0)

    xb = gather_rows(t, buf_tok, bm)
    yb = moe_experts(xb, block_e, n_used, buf_w, wg.astype(BF16), wu.astype(BF16), wd.astype(BF16), bm)
    return moe_combine(h, yb, pos)


def kernel(x, rwkv_norm_g, rwkv_mix, rwkv_w_r, rwkv_w_k, rwkv_w_v, rwkv_w0, rwkv_w1, rwkv_w2, rwkv_a0, rwkv_a1,
           rwkv_a2, rwkv_g1, rwkv_g2, rwkv_k_k, rwkv_k_a, rwkv_r_k, rwkv_gn_g, rwkv_gn_b, rwkv_w_o, kv_norm_g,
           w_kv, k_norm_g, attn_norm_g, attn_w_q, q_norm_g, lambda_q1, lambda_k1, lambda_q2, lambda_k2, subln_g,
           attn_w_o, moe_norm_g, router_group_w, router_group_b, router_expert_w, router_expert_b, expert_w_gate,
           expert_w_up, expert_w_down):
    batch, seq, d = x.shape
    depth = moe_norm_g.shape[0]
    n_a = rwkv_norm_g.shape[0]
    h = x.reshape(batch * seq, d)
    k_shared = v_shared = None
    for l in range(depth):
        if l < n_a:
            i = l
            h = rwkv_layer(h, batch, seq, rwkv_norm_g[i], rwkv_mix[i], rwkv_w_r[i], rwkv_w_k[i], rwkv_w_v[i],
                           rwkv_w0[i], rwkv_w1[i], rwkv_w2[i], rwkv_a0[i], rwkv_a1[i], rwkv_a2[i], rwkv_g1[i],
                           rwkv_g2[i], rwkv_k_k[i], rwkv_k_a[i], rwkv_r_k[i], rwkv_gn_g[i], rwkv_gn_b[i],
                           rwkv_w_o[i])
        else:
            j = l - n_a
            if j == 0:
                k_shared, v_shared = shared_kv(h, kv_norm_g, w_kv, k_norm_g)
            h = attn_layer(h, k_shared, v_shared, batch, seq, l, attn_norm_g[j], attn_w_q[j], q_norm_g[j], k_norm_g,
                           lambda_q1[j], lambda_k1[j], lambda_q2[j], lambda_k2[j], subln_g[j], attn_w_o[j])
        h = moe_layer(h, moe_norm_g[l], router_group_w[l], router_group_b[l], router_expert_w[l],
                      router_expert_b[l], expert_w_gate[l], expert_w_up[l], expert_w_down[l])
    return h.reshape(batch, seq, d)
```

```python
import functools
import math

import jax
import jax.numpy as jnp
from jax import lax
from jax.experimental import pallas as pl
from jax.experimental.pallas import tpu as pltpu

V7X_LANES = 128
V7X_SUBLANES = 8
V7X_VMEM_BYTES = 64 * 1024 * 1024
VMEM_LIMIT_BYTES = V7X_VMEM_BYTES - 8 * 1024 * 1024

RWKV_HEAD = 64
DIFF_HEAD = 64
TOP_K = 2
GN_EPS = 64e-5
RMS_EPS = 1e-6
WKV_CHUNK = 64
WKV_PAIRS = 16

BF16 = jnp.bfloat16
F32 = jnp.float32


def _tile(n, pref):
    t = min(n, pref)
    assert n % t == 0, (n, pref)
    return t


def _params(*sem):
    return pltpu.CompilerParams(dimension_semantics=sem, vmem_limit_bytes=VMEM_LIMIT_BYTES)


def _mm(a, b):
    return jnp.dot(a.astype(BF16), b.astype(BF16), preferred_element_type=F32)


def _mm_nt(a, b):
    return lax.dot_general(a.astype(BF16), b.astype(BF16), (((1,), (1,)), ((), ())), preferred_element_type=F32)


def _split_dot(x, w_bf16):
    hi = x.astype(BF16)
    lo = (x - hi.astype(F32)).astype(BF16)
    return (jnp.dot(hi, w_bf16, preferred_element_type=F32) + jnp.dot(lo, w_bf16, preferred_element_type=F32))


def _seg_ones(n, seg):
    i = lax.broadcasted_iota(jnp.int32, (n, n), 0) // seg
    j = lax.broadcasted_iota(jnp.int32, (n, n), 1) // seg
    return jnp.where(i == j, 1.0, 0.0).astype(BF16)


def _rms(x, g):
    return x * lax.rsqrt(jnp.mean(x * x, axis=-1, keepdims=True) + RMS_EPS) * g


def _rmsnorm_kernel(x_ref, g_ref, o_ref):
    o_ref[...] = _rms(x_ref[...], g_ref[...]).astype(o_ref.dtype)


def rmsnorm(x, g, out_dtype=BF16):
    n, d = x.shape
    tm = _tile(n, 512)
    return pl.pallas_call(
        _rmsnorm_kernel,
        grid=(n // tm,),
        in_specs=[pl.BlockSpec((tm, d), lambda i: (i, 0)), pl.BlockSpec((1, d), lambda i: (0, 0))],
        out_specs=pl.BlockSpec((tm, d), lambda i: (i, 0)),
        out_shape=jax.ShapeDtypeStruct((n, d), out_dtype),
        compiler_params=_params("parallel"),
        name="rmsnorm",
    )(x, g.reshape(1, d))


def _rwkv_prep_kernel(x_ref, prev_ref, g_ref, mix_ref, o_ref, *, tm, seq):
    i = pl.program_id(0)
    g = g_ref[...]
    xn = _rms(x_ref[...], g)
    prev = _rms(prev_ref[...], g)[V7X_SUBLANES - 1:V7X_SUBLANES]
    prev = jnp.where((i * tm) % seq == 0, 0.0, prev)
    row = lax.broadcasted_iota(jnp.int32, xn.shape, 0)
    shifted = jnp.where(row == 0, prev, pltpu.roll(xn, 1, 0))
    xx = shifted - xn
    for s in range(o_ref.shape[0]):
        o_ref[s] = (xn + xx * mix_ref[s:s + 1, :]).astype(o_ref.dtype)


def rwkv_prep(h, g, mix, seq):
    n, d = h.shape
    n_mix = mix.shape[0]
    tm = _tile(seq, 256)
    per = tm // V7X_SUBLANES
    return pl.pallas_call(
        functools.partial(_rwkv_prep_kernel, tm=tm, seq=seq),
        grid=(n // tm,),
        in_specs=[
            pl.BlockSpec((tm, d), lambda i: (i, 0)),
            pl.BlockSpec((V7X_SUBLANES, d), lambda i: (jnp.maximum(i * per - 1, 0), 0)),
            pl.BlockSpec((1, d), lambda i: (0, 0)),
            pl.BlockSpec((n_mix, d), lambda i: (0, 0)),
        ],
        out_specs=pl.BlockSpec((n_mix, tm, d), lambda i: (0, i, 0)),
        out_shape=jax.ShapeDtypeStruct((n_mix, n, d), BF16),
        compiler_params=_params("parallel"),
        name="rwkv_prep",
    )(h, h, g.reshape(1, d), mix)


def _matmul_kernel(x_ref, w_ref, *rest, epilogue, n_extra):
    extra = [r[...] for r in rest[:n_extra]]
    o_ref = rest[n_extra]
    acc = jnp.dot(x_ref[...], w_ref[...], preferred_element_type=F32)
    o_ref[...] = epilogue(acc, *extra).astype(o_ref.dtype)


def matmul(x, w, *, out_dtype, epilogue=None, rows=(), tiles=(), x_sel=None, w_cols=None, tm=1024, tn=1024,
           name="matmul"):
    if x_sel is None:
        m, k = x.shape
        x_spec = lambda tm_: pl.BlockSpec((tm_, k), lambda i, j: (i, 0))
    else:
        _, m, k = x.shape
        x_spec = lambda tm_: pl.BlockSpec((None, tm_, k), lambda i, j: (x_sel, i, 0))
    col0, n_out = (0, w.shape[1]) if w_cols is None else w_cols
    tm = _tile(m, tm)
    tn = _tile(n_out, tn)
    assert col0 % tn == 0
    j0 = col0 // tn
    if epilogue is None:
        epilogue = lambda acc: acc
    in_specs = [x_spec(tm), pl.BlockSpec((k, tn), lambda i, j: (0, j + j0))]
    in_specs += [pl.BlockSpec((1, tn), lambda i, j: (0, j)) for _ in rows]
    in_specs += [pl.BlockSpec((tm, tn), lambda i, j: (i, j)) for _ in tiles]
    return pl.pallas_call(
        functools.partial(_matmul_kernel, epilogue=epilogue, n_extra=len(rows) + len(tiles)),
        grid=(m // tm, n_out // tn),
        in_specs=in_specs,
        out_specs=pl.BlockSpec((tm, tn), lambda i, j: (i, j)),
        out_shape=jax.ShapeDtypeStruct((m, n_out), out_dtype),
        compiler_params=_params("parallel", "arbitrary"),
        name=name,
    )(x, w, *[r.reshape(1, n_out).astype(F32) for r in rows], *tiles)


def _seg_rms_epilogue(acc, g, *, seg, scale):
    ones = _seg_ones(V7X_LANES, seg)
    outs = []
    for s in range(acc.shape[1] // V7X_LANES):
        a = acc[:, s * V7X_LANES:(s + 1) * V7X_LANES]
        ss = _split_dot(a * a, ones)
        outs.append(a * lax.rsqrt(ss * (1.0 / seg) + RMS_EPS))
    return jnp.concatenate(outs, axis=1) * (g * scale)


def _wkv_kernel(r_ref, lw_ref, k_ref, v_ref, a_ref, g_ref, kk_ref, ka_ref, rk_ref, gg_ref, gb_ref, o_ref, h_ref,
                *, chunk, pairs):
    c2 = 2 * chunk
    half = V7X_LANES // 2
    assert c2 == V7X_LANES, "two stacked heads of one chunk must fill one 128-row tile"

    @pl.when(pl.program_id(2) == 0)
    def _():
        h_ref[...] = jnp.zeros_like(h_ref)

    lane = lax.broadcasted_iota(jnp.int32, (chunk, V7X_LANES), 1)
    head0 = lane < half
    ri = lax.broadcasted_iota(jnp.int32, (c2, c2), 0)
    ci = lax.broadcasted_iota(jnp.int32, (c2, c2), 1)
    same = (ri // chunk) == (ci // chunk)
    strict = jnp.logical_and(same, (ci % chunk) < (ri % chunk))
    incl = jnp.logical_and(same, (ci % chunk) <= (ri % chunk))
    eye = jnp.where(ri == ci, 1.0, 0.0)
    li = lax.broadcasted_iota(jnp.int32, (chunk, chunk), 0)
    lj = lax.broadcasted_iota(jnp.int32, (chunk, chunk), 1)
    ltri = jnp.where(li >= lj, 1.0, 0.0).astype(BF16)
    seg = _seg_ones(V7X_LANES, RWKV_HEAD)
    n_double = int(math.log2(chunk)) - 1

    def stack(x):
        return jnp.concatenate([jnp.where(head0, x, 0.0), jnp.where(head0, 0.0, x)], axis=0)

    def hi_lo(x, axis):
        hi = x.astype(BF16)
        return jnp.concatenate([hi, (x - hi.astype(F32)).astype(BF16)], axis=axis)

    def seg_sums(xs):
        prods = [jnp.dot(hi_lo(x, 0), seg, preferred_element_type=F32) for x in xs]
        return [o[:chunk] + o[chunk:] for o in prods]

    lanes = V7X_LANES
    P = range(pairs)
    cols = [slice(p * lanes, (p + 1) * lanes) for p in P]
    zeros = jnp.zeros((c2, lanes), F32)

    def body(c, carry):
        rows = pl.ds(pl.multiple_of(c * chunk, chunk), chunk)
        ld = lambda ref: [ref[rows, cs].astype(F32) for cs in cols]
        par = lambda ref: [ref[:, cs] for cs in cols]
        r, lw, k, v, a = ld(r_ref), ld(lw_ref), ld(k_ref), ld(v_ref), ld(a_ref)
        k_k, k_a, r_k = par(kk_ref), par(ka_ref), par(rk_ref)

        kk = [k[p] * k_k[p] for p in P]
        ss = seg_sums([x * x for x in kk])
        kk = [kk[p] * lax.rsqrt(jnp.maximum(ss[p], 1e-24)) for p in P]
        k = [k[p] * (1.0 + (a[p] - 1.0) * k_a[p]) for p in P]
        b = [kk[p] * a[p] for p in P]
        bonus = seg_sums([r[p] * k[p] * r_k[p] for p in P])

        cc = [jnp.dot(ltri, hi_lo(lw[p], 1), preferred_element_type=F32) for p in P]
        cum = [x[:, :lanes] + x[:, lanes:] for x in cc]
        mid = [x[chunk // 2 - 1:chunk // 2] for x in cum]
        last = [x[chunk - 1:chunk] for x in cum]
        dec_mid = [jnp.exp(mid[p] - cum[p]) for p in P]
        dec_last = [jnp.exp(last[p] - cum[p]) for p in P]
        emid = [jnp.exp(x) for x in mid]
        plast = [jnp.exp(x) for x in last]
        As = [stack(-kk[p] * jnp.exp(cum[p] - lw[p] - mid[p])) for p in P]
        Rs = [stack(r[p] * jnp.exp(cum[p] - mid[p])) for p in P]
        Vs = [stack(v[p]) for p in P]
        AR = [jnp.concatenate([As[p], Rs[p]], axis=0) for p in P]
        BK = [jnp.concatenate([stack(b[p] * dec_mid[p]), stack(k[p] * dec_mid[p])], axis=0) for p in P]
        BKh = [jnp.concatenate([stack(b[p] * dec_last[p]), stack(k[p] * dec_last[p])], axis=0) for p in P]

        G = [_mm_nt(AR[p], BK[p]) for p in P]
        Aab = [jnp.where(strict, g[:c2, :c2], 0.0) for g in G]
        Aak = [jnp.where(strict, g[:c2, c2:], 0.0) for g in G]
        S = [jnp.concatenate([jnp.where(incl, g[c2:, :c2], 0.0), jnp.where(incl, g[c2:, c2:], 0.0)], axis=1)
             for g in G]

        X = [eye + x for x in Aab]
        Q = [_mm(x, x) for x in Aab]
        for _ in range(n_double - 1):
            QX = [_mm(Q[p], jnp.concatenate([Q[p], X[p]], axis=1)) for p in P]
            X = [X[p] + QX[p][:, lanes:] for p in P]
            Q = [x[:, :lanes] for x in QX]
        X = [X[p] + _mm(Q[p], X[p]) for p in P]

        AV = [_mm(Aak[p], Vs[p]) for p in P]
        TA = [_mm(X[p], jnp.concatenate([As[p], AV[p]], axis=1)) for p in P]
        lowV = [jnp.concatenate([zeros, Vs[p]], axis=1) for p in P]
        SY = [_mm(S[p], jnp.concatenate([TA[p], lowV[p]], axis=0)) for p in P]
        Rbar = [(Rs[p] + SY[p][:, :lanes]) * emid[p] for p in P]
        AU = [jnp.concatenate([TA[p][:, :lanes] * emid[p], TA[p][:, lanes:]], axis=1) for p in P]
        BKT = [jnp.transpose(x) for x in BKh]
        MN = [_mm(BKT[p], jnp.concatenate([AU[p], lowV[p]], axis=0)) for p in P]
        M = [eye * plast[p] + MN[p][:, :lanes] for p in P]
        H = [h_ref[p] for p in P]
        RH = [_mm(jnp.concatenate([Rbar[p], M[p]], axis=0), H[p]) for p in P]
        for p in P:
            h_ref[p] = RH[p][c2:] + MN[p][:, lanes:]
        Ys = [RH[p][:c2] + SY[p][:, lanes:] for p in P]
        y = [x[:chunk] + x[chunk:] for x in Ys]

        inv = 1.0 / RWKV_HEAD
        mean = seg_sums(y)
        yc = [y[p] - mean[p] * inv for p in P]
        var = seg_sums([x * x for x in yc])
        gn_g, gn_b = par(gg_ref), par(gb_ref)
        for p in P:
            yn = yc[p] * lax.rsqrt(var[p] * inv + GN_EPS) * gn_g[p] + gn_b[p]
            gate = g_ref[rows, cols[p]].astype(F32)
            o_ref[rows, cols[p]] = ((yn + bonus[p] * v[p]) * gate).astype(o_ref.dtype)
        return carry

    lax.fori_loop(0, r_ref.shape[0] // chunk, body, 0)


def wkv7(r, lw, k, v, a, g, k_k, k_a, r_k, gn_g, gn_b, batch, seq):
    n, d = r.shape
    pairs = min(WKV_PAIRS, d // V7X_LANES)
    width = pairs * V7X_LANES
    tb = _tile(seq, 256)
    seq_spec = pl.BlockSpec((tb, width), lambda b, p, t: (b * (seq // tb) + t, p))
    par_spec = pl.BlockSpec((1, width), lambda b, p, t: (0, p))
    row = lambda x: x.reshape(1, d).astype(F32)
    return pl.pallas_call(
        functools.partial(_wkv_kernel, chunk=WKV_CHUNK, pairs=pairs),
        grid=(batch, d // width, seq // tb),
        in_specs=[seq_spec] * 6 + [par_spec] * 5,
        out_specs=seq_spec,
        out_shape=jax.ShapeDtypeStruct((n, d), BF16),
        scratch_shapes=[pltpu.VMEM((pairs, V7X_LANES, V7X_LANES), F32)],
        compiler_params=_params("parallel", "parallel", "arbitrary"),
        name="wkv7",
    )(r, lw, k, v, a, g, row(k_k), row(k_a), row(r_k), row(gn_g), row(gn_b))


def _decay_epilogue(acc, w0):
    z = -(w0 + acc)
    softplus = jnp.maximum(z, 0.0) + jnp.log(1.0 + jnp.exp(-jnp.abs(z)))
    return -jnp.exp(-softplus - 0.5)


def rwkv_layer(h, batch, seq, norm_g, mix, w_r, w_k, w_v, w0, w1, w2, a0, a1, a2, g1, g2, k_k, k_a, r_k, gn_g,
               gn_b, w_o):
    d = h.shape[1]
    xs = rwkv_prep(h, norm_g, mix, seq)
    bf = lambda w: w.astype(BF16)

    def pad_cols(w):
        return jnp.pad(w, ((0, 0), (0, -w.shape[1] % V7X_LANES)))

    def pad_rows(w):
        return jnp.pad(w, ((0, -w.shape[0] % V7X_LANES), (0, 0)))

    r = matmul(xs, bf(w_r), x_sel=0, out_dtype=BF16, name="rwkv_r")
    k = matmul(xs, bf(w_k), x_sel=2, out_dtype=BF16, name="rwkv_k")
    v = matmul(xs, bf(w_v), x_sel=3, out_dtype=BF16, name="rwkv_v")
    tw = matmul(xs, bf(pad_cols(w1)), x_sel=1, out_dtype=BF16, epilogue=jnp.tanh, name="rwkv_w1")
    lw = matmul(tw, bf(pad_rows(w2)), out_dtype=F32, rows=(w0,), name="rwkv_w2",
                epilogue=_decay_epilogue)
    ta = matmul(xs, bf(pad_cols(a1)), x_sel=4, out_dtype=BF16, name="rwkv_a1")
    a = matmul(ta, bf(pad_rows(a2)), out_dtype=BF16, rows=(a0,), name="rwkv_a2",
               epilogue=lambda acc, b: jax.nn.sigmoid(b + acc))
    tg = matmul(xs, bf(pad_cols(g1)), x_sel=5, out_dtype=BF16, epilogue=jax.nn.sigmoid, name="rwkv_g1")
    g = matmul(tg, bf(pad_rows(g2)), out_dtype=BF16, name="rwkv_g2")
    y = wkv7(r, lw, k, v, a, g, k_k, k_a, r_k, gn_g, gn_b, batch, seq)
    return matmul(y, bf(w_o), out_dtype=F32, tiles=(h,), epilogue=lambda acc, res: res + acc, name="rwkv_o")


LOG2_E = 1.4426950408889634
ATTN_UNSHIFTED_MAX_LOG2 = 80.0
ATTN_UNROLL = 4
ATTN_LAG = 1
ATTN_SUM_ROWS = 16


def _sub_head_queries(q):
    lane = lax.broadcasted_iota(jnp.int32, q.shape, 1)
    zero = jnp.zeros_like(q)
    return jnp.where(lane < DIFF_HEAD, q, zero), jnp.where(lane < DIFF_HEAD, zero, q)


def _causal_tile(tq):
    return lax.broadcasted_iota(jnp.int32, (tq, tq), 0) >= lax.broadcasted_iota(jnp.int32, (tq, tq), 1)


def _attn_online_kernel(lam_ref, q_ref, k_ref, v_ref, sg_ref, o_ref, *, tq, out_scale):
    qi = pl.program_id(2)
    qs = _sub_head_queries(q_ref[...])
    causal = _causal_tile(tq)

    def step(j, carry, masked):
        start = pl.multiple_of(j * tq, tq)
        kj = k_ref[pl.ds(start, tq), :]
        vj = v_ref[pl.ds(start, tq), :]
        out = []
        for sub in range(2):
            m, l, acc = carry[sub]
            s = lax.dot_general(qs[sub], kj, (((1,), (1,)), ((), ())), preferred_element_type=F32)
            if masked:
                s = jnp.where(causal, s, -1e30)
            m_new = jnp.maximum(m, jnp.max(s, axis=-1, keepdims=True))
            alpha = jnp.exp2(m - m_new)
            p = jnp.exp2(s - m_new)
            l = alpha * l + jnp.sum(p, axis=-1, keepdims=True)
            acc = alpha * acc + jnp.dot(p.astype(BF16), vj, preferred_element_type=F32)
            out.append((m_new, l, acc))
        return tuple(out)

    init = tuple((jnp.full((tq, 1), -1e30, F32), jnp.zeros((tq, 1), F32), jnp.zeros((tq, V7X_LANES), F32))
                 for _ in range(2))
    carry = lax.fori_loop(0, qi, lambda j, c: step(j, c, False), init)
    (_, l0, acc0), (_, l1, acc1) = step(qi, carry, True)
    o = acc0 / l0 - lam_ref[0] * (acc1 / l1)
    o_ref[...] = (_rms(o, sg_ref[...]) * out_scale).astype(o_ref.dtype)


def _attn_unshifted_kernel(lam_ref, q_ref, k_ref, vt_ref, sg_ref, o_ref, acc_ref, *, tq, out_scale):
    qi = pl.program_id(2)
    qs = _sub_head_queries(q_ref[...])
    visible = lax.broadcasted_iota(jnp.int32, (tq, tq), 1) >= lax.broadcasted_iota(jnp.int32, (tq, tq), 0)
    ones = jnp.ones((ATTN_SUM_ROWS, tq), BF16)
    acc_ref[...] = jnp.zeros_like(acc_ref)

    def scores(j, sub, masked):
        kj = k_ref[pl.ds(pl.multiple_of(j * tq, tq), tq), :]
        st = lax.dot_general(kj, qs[sub], (((1,), (1,)), ((), ())), preferred_element_type=F32)
        return jnp.where(visible, st, -1e30) if masked else st

    def accumulate(j, sub, st):
        vt = jnp.concatenate([vt_ref[j], ones], axis=0)
        acc_ref[sub] += jnp.dot(vt, jnp.exp2(st).astype(BF16), preferred_element_type=F32)

    def steps(tiles):
        work = [(j, sub, masked) for j, masked in tiles for sub in range(2)]
        pending = []
        for j, sub, masked in work:
            pending.append((j, sub, scores(j, sub, masked)))
            if len(pending) > ATTN_LAG:
                accumulate(*pending.pop(0))
        for item in pending:
            accumulate(*item)

    def body(jj, c):
        steps([(jj * ATTN_UNROLL + u, False) for u in range(ATTN_UNROLL)])
        return c

    n_full = qi // ATTN_UNROLL
    lax.fori_loop(0, n_full, body, 0)
    first = n_full * ATTN_UNROLL
    for rem in range(ATTN_UNROLL):
        @pl.when(qi - first == rem)
        def _():
            steps([(first + u, False) for u in range(rem)] + [(qi, True)])
    a0 = acc_ref[0]
    a1 = acc_ref[1]
    hw = 2 * DIFF_HEAD
    ot = a0[:hw] / a0[hw:hw + 1] - lam_ref[0] * (a1[:hw] / a1[hw:hw + 1])
    ot = ot * lax.rsqrt(jnp.mean(ot * ot, axis=0, keepdims=True) + RMS_EPS)
    o_ref[...] = (jnp.transpose(ot) * (sg_ref[...] * out_scale)).astype(o_ref.dtype)


def diff_attention_core(q, k, v, lam, subln_g, batch, seq, lam_init, unshifted):
    n, d = q.shape
    tq = _tile(seq, 512)
    nq = seq // tq
    head_w = 2 * DIFF_HEAD
    q_spec = pl.BlockSpec((tq, head_w), lambda b, h, i: (b * nq + i, h))
    kv_spec = pl.BlockSpec((seq, head_w), lambda b, h, i: (b, h))
    if unshifted:
        body = _attn_unshifted_kernel
        scratch = [pltpu.VMEM((2, head_w + ATTN_SUM_ROWS, tq), F32)]
        v = jnp.transpose(v.reshape(n // tq, tq, d), (0, 2, 1))
        v_spec = pl.BlockSpec((nq, head_w, tq), lambda b, h, i: (b, h, 0))
    else:
        body = _attn_online_kernel
        scratch = []
        v_spec = kv_spec
    return pl.pallas_call(
        functools.partial(body, tq=tq, out_scale=1.0 - lam_init),
        grid=(batch, d // head_w, nq),
        in_specs=[pl.BlockSpec(memory_space=pltpu.SMEM), q_spec, kv_spec, v_spec,
                  pl.BlockSpec((1, head_w), lambda b, h, i: (0, 0))],
        out_specs=q_spec,
        out_shape=jax.ShapeDtypeStruct((n, d), BF16),
        scratch_shapes=scratch,
        compiler_params=_params("parallel", "parallel", "arbitrary"),
        name="diff_attn_unshifted" if unshifted else "diff_attn_online",
    )(lam.reshape(1).astype(F32), q, k, v, subln_g.reshape(1, head_w).astype(F32))


def shared_kv(h, kv_norm_g, w_kv, k_norm_g):
    d = h.shape[1]
    hn = rmsnorm(h, kv_norm_g)
    kg = jnp.tile(k_norm_g, d // DIFF_HEAD)
    w = w_kv.astype(BF16)
    k = matmul(hn, w, w_cols=(0, d), out_dtype=BF16, rows=(kg,), name="kv_k", tn=512,
               epilogue=functools.partial(_seg_rms_epilogue, seg=DIFF_HEAD, scale=1.0))
    v = matmul(hn, w, w_cols=(d, d), out_dtype=BF16, name="kv_v")
    return k, v


def attn_layer(h, k, v, batch, seq, layer_idx, norm_g, w_q, q_norm_g, k_norm_g, lq1, lk1, lq2, lk2, subln_g, w_o):
    d = h.shape[1]
    hn = rmsnorm(h, norm_g)
    qg = jnp.tile(q_norm_g, d // DIFF_HEAD)
    q_scale = LOG2_E * DIFF_HEAD ** -0.5
    q = matmul(hn, w_q.astype(BF16), out_dtype=BF16, rows=(qg,), name="attn_q", tn=512,
               epilogue=functools.partial(_seg_rms_epilogue, seg=DIFF_HEAD, scale=q_scale))
    lam_init = 0.8 - 0.6 * math.exp(-0.3 * layer_idx)
    lam = jnp.exp(jnp.sum(lq1 * lk1)) - jnp.exp(jnp.sum(lq2 * lk2)) + lam_init
    bound = 1.01 * q_scale * DIFF_HEAD * jnp.max(jnp.abs(q_norm_g)) * jnp.max(jnp.abs(k_norm_g))
    core = lambda unshifted: functools.partial(diff_attention_core, batch=batch, seq=seq, lam_init=lam_init,
                                               unshifted=unshifted)
    o = lax.cond(bound <= ATTN_UNSHIFTED_MAX_LOG2, core(True), core(False), q, k, v, lam, subln_g)
    return matmul(o, w_o.astype(BF16), out_dtype=F32, tiles=(h,), epilogue=lambda acc, res: res + acc,
                  name="attn_o")


PACK_ROWS = V7X_SUBLANES


def _pack_store(x, o_ref):
    m = x.shape[0]
    assert x.shape[1] == 2 * PACK_ROWS * V7X_LANES
    word = lambda j: pltpu.bitcast(x[:, j * V7X_LANES:(j + 1) * V7X_LANES].astype(BF16).astype(F32), jnp.uint32)
    for s in range(PACK_ROWS):
        o_ref[pl.ds(s, m, stride=PACK_ROWS), :] = word(s) | (word(s + PACK_ROWS) >> 16)


def _unpack_load(p_ref, m):
    words = [p_ref[pl.ds(s, m, stride=PACK_ROWS), :] for s in range(PACK_ROWS)]
    hi = [pltpu.bitcast(w & jnp.uint32(0xFFFF0000), F32) for w in words]
    lo = [pltpu.bitcast(w << 16, F32) for w in words]
    return jnp.concatenate(hi + lo, axis=1)


ROUTE_ROWS = V7X_SUBLANES


def _router_kernel(x_ref, g_ref, w_ref, b_ref, t_ref, route_ref, *, n_groups, per_group):
    t = _rms(x_ref[...], g_ref[...])
    _pack_store(t, t_ref)
    logits = jnp.dot(t, w_ref[...], preferred_element_type=F32, precision=lax.Precision.HIGHEST) + b_ref[...]
    lane = lax.broadcasted_iota(jnp.int32, logits.shape, 1)
    none = jnp.int32(logits.shape[1])
    neg = -jnp.inf
    rmax = lambda x: jnp.max(x, axis=-1, keepdims=True)
    first = lambda hit: jnp.min(jnp.where(hit, lane, none), axis=-1, keepdims=True)

    is_group = lane < n_groups
    g_logit = jnp.where(is_group, logits, neg)
    g_max = rmax(g_logit)
    grp = first(g_logit == g_max)
    p_group = 1.0 / jnp.sum(jnp.where(is_group, jnp.exp(logits - g_max), 0.0), axis=-1, keepdims=True)

    lo = n_groups + grp * per_group
    e_logit = jnp.where(jnp.logical_and(lane >= lo, lane < lo + per_group), logits, neg)
    top1 = rmax(e_logit)
    i1 = first(e_logit == top1)
    e_rest = jnp.where(lane == i1, neg, e_logit)
    top2 = rmax(e_rest)
    i2 = first(e_rest == top2)
    ratio = jnp.exp(top2 - top1)
    gate1 = p_group / (1.0 + ratio)
    gate2 = gate1 * ratio

    cols = [(i1 - n_groups).astype(F32), (i2 - n_groups).astype(F32), gate1, gate2]
    out = jnp.zeros_like(logits)
    for c, val in enumerate(cols):
        out = jnp.where(lane == c, val, out)
    route_ref[...] = jnp.transpose(out)[:ROUTE_ROWS]


def moe_router(h, norm_g, rg_w, rg_b, re_w, re_b):
    n, d = h.shape
    tm = _tile(n, 256)
    n_groups, n_exp = rg_w.shape[1], re_w.shape[1]
    assert n_groups + n_exp <= V7X_LANES
    pad = V7X_LANES - n_groups - n_exp
    w = jnp.pad(jnp.concatenate([rg_w, re_w], axis=1), ((0, 0), (0, pad)))
    b = jnp.pad(jnp.concatenate([rg_b, re_b]), (0, pad)).reshape(1, -1)
    return pl.pallas_call(
        functools.partial(_router_kernel, n_groups=n_groups, per_group=n_exp // n_groups),
        grid=(n // tm,),
        in_specs=[pl.BlockSpec((tm, d), lambda i: (i, 0)), pl.BlockSpec((1, d), lambda i: (0, 0)),
                  pl.BlockSpec((d, V7X_LANES), lambda i: (0, 0)), pl.BlockSpec((1, V7X_LANES), lambda i: (0, 0))],
        out_specs=[pl.BlockSpec((tm * PACK_ROWS, V7X_LANES), lambda i: (i, 0)),
                   pl.BlockSpec((ROUTE_ROWS, tm), lambda i: (0, i))],
        out_shape=[jax.ShapeDtypeStruct((n * PACK_ROWS, V7X_LANES), jnp.uint32),
                   jax.ShapeDtypeStruct((ROUTE_ROWS, n), F32)],
        compiler_params=_params("parallel"),
        name="moe_router",
    )(h, norm_g.reshape(1, d), w, b)


def _token_copy(src_hbm, token, dst, slot, sem):
    src = src_hbm.at[pl.ds(pl.multiple_of(token * PACK_ROWS, PACK_ROWS), PACK_ROWS), :]
    return pltpu.make_async_copy(src, dst.at[pl.ds(pl.multiple_of(slot * PACK_ROWS, PACK_ROWS), PACK_ROWS), :], sem)


DMA_LOOP_UNROLL = 8
DMA_PRIORITIES = 2


def _gather_kernel(idx_ref, x_hbm, o_ref, sem, *, rows):
    def issue(r2, c):
        for prio in range(DMA_PRIORITIES):
            r = r2 * DMA_PRIORITIES + prio
            _token_copy(x_hbm, idx_ref[0, 0, r], o_ref, r, sem).start(priority=prio)
        return c

    def drain(r, c):
        _token_copy(x_hbm, 0, o_ref, r, sem).wait()
        return c

    lax.fori_loop(0, rows // DMA_PRIORITIES, issue, 0, unroll=DMA_LOOP_UNROLL // DMA_PRIORITIES)
    lax.fori_loop(0, rows, drain, 0, unroll=DMA_LOOP_UNROLL)


def gather_rows(x, idx, rows_per_step=256):
    n_out = idx.shape[0]
    bm = _tile(n_out, rows_per_step)
    return pl.pallas_call(
        functools.partial(_gather_kernel, rows=bm),
        grid=(n_out // bm,),
        in_specs=[pl.BlockSpec((1, 1, bm), lambda i: (i, 0, 0), memory_space=pltpu.SMEM),
                  pl.BlockSpec(memory_space=pl.ANY)],
        out_specs=pl.BlockSpec((bm * PACK_ROWS, V7X_LANES), lambda i: (i, 0)),
        out_shape=jax.ShapeDtypeStruct((n_out * PACK_ROWS, V7X_LANES), x.dtype),
        scratch_shapes=[pltpu.SemaphoreType.DMA(())],
        compiler_params=_params("arbitrary"),
        name="moe_gather",
    )(idx.reshape(n_out // bm, 1, bm), x)


EXPERT_WEIGHT_CHUNK_BYTES = 1 << 20


def _expert_kernel(be_ref, nused_ref, x_ref, wg_hbm, wu_hbm, wd_hbm, rw_ref, o_ref,
                   wg_v, wu_v, wd_v, stage_in, stage_out, sem, *, bm, layer):
    i = pl.program_id(0)
    e = be_ref[i]
    used = i < nused_ref[0]
    new_expert = jnp.logical_or(i == 0, e != be_ref[jnp.maximum(i - 1, 0)])

    def load(w_hbm, w_v, stage):
        chunk = stage.shape[1]
        n_chunks = w_v.shape[0] // chunk
        copy = lambda c: pltpu.make_async_copy(w_hbm.at[layer, e, pl.ds(c * chunk, chunk), :], stage.at[c % 2],
                                               sem.at[c % 2])
        copy(0).start()
        for c in range(n_chunks):
            if c + 1 < n_chunks:
                copy(c + 1).start()
            copy(c).wait()
            w_v[pl.ds(c * chunk, chunk), :] = stage[c % 2].astype(BF16)

    @pl.when(jnp.logical_and(used, new_expert))
    def _():
        load(wg_hbm, wg_v, stage_in)
        load(wu_hbm, wu_v, stage_in)
        load(wd_hbm, wd_v, stage_out)

    @pl.when(used)
    def _():
        x = _unpack_load(x_ref, bm).astype(BF16)
        hg = jnp.dot(x, wg_v[...], preferred_element_type=F32)
        hu = jnp.dot(x, wu_v[...], preferred_element_type=F32)
        act = (hg * jax.nn.sigmoid(hg) * hu).astype(BF16)
        _pack_store(jnp.dot(act, wd_v[...], preferred_element_type=F32) * rw_ref[...], o_ref)

    @pl.when(jnp.logical_not(used))
    def _():
        o_ref[...] = jnp.zeros_like(o_ref)


def moe_experts(xb, block_e, n_used, row_w, wg, wu, wd, layer, bm):
    n_pad = row_w.shape[0]
    d, hid = wg.shape[2], wg.shape[3]
    chunk_in = EXPERT_WEIGHT_CHUNK_BYTES // (hid * 4)
    chunk_out = EXPERT_WEIGHT_CHUNK_BYTES // (d * 4)
    assert d % chunk_in == 0 and hid % chunk_out == 0
    packed = pl.BlockSpec((bm * PACK_ROWS, V7X_LANES), lambda i, be, nu: (i, 0))
    hbm = pl.BlockSpec(memory_space=pl.ANY)
    return pl.pallas_call(
        functools.partial(_expert_kernel, bm=bm, layer=layer),
        grid_spec=pltpu.PrefetchScalarGridSpec(
            num_scalar_prefetch=2,
            grid=(n_pad // bm,),
            in_specs=[packed, hbm, hbm, hbm, pl.BlockSpec((bm, 1), lambda i, be, nu: (i, 0))],
            out_specs=packed,
            scratch_shapes=[
                pltpu.VMEM((d, hid), BF16), pltpu.VMEM((d, hid), BF16), pltpu.VMEM((hid, d), BF16),
                pltpu.VMEM((2, chunk_in, hid), F32), pltpu.VMEM((2, chunk_out, d), F32),
                pltpu.SemaphoreType.DMA((2,)),
            ],
        ),
        out_shape=jax.ShapeDtypeStruct(xb.shape, xb.dtype),
        compiler_params=_params("arbitrary"),
        name="moe_experts",
    )(block_e, n_used, xb, wg, wu, wd, row_w.reshape(n_pad, 1))


def _combine_kernel(idx_ref, yb_hbm, h_ref, o_ref, buf, sem, *, rows):
    def issue(r, c):
        for s in range(TOP_K):
            _token_copy(yb_hbm, idx_ref[0, s, r], buf.at[s], r, sem).start(priority=s % DMA_PRIORITIES)
        return c

    def drain(r, c):
        for s in range(TOP_K):
            _token_copy(yb_hbm, 0, buf.at[s], r, sem).wait()
        return c

    lax.fori_loop(0, rows, issue, 0, unroll=DMA_LOOP_UNROLL)
    lax.fori_loop(0, rows, drain, 0, unroll=DMA_LOOP_UNROLL)
    acc = h_ref[...]
    for s in range(TOP_K):
        acc = acc + _unpack_load(buf.at[s], rows)
    o_ref[...] = acc


def moe_combine(h, yb, pos, rows_per_step=256):
    n, d = h.shape
    tc = _tile(n, rows_per_step)
    idx = pos.reshape(n // tc, tc, TOP_K).transpose(0, 2, 1)
    return pl.pallas_call(
        functools.partial(_combine_kernel, rows=tc),
        grid=(n // tc,),
        in_specs=[pl.BlockSpec((1, TOP_K, tc), lambda i: (i, 0, 0), memory_space=pltpu.SMEM),
                  pl.BlockSpec(memory_space=pl.ANY),
                  pl.BlockSpec((tc, d), lambda i: (i, 0))],
        out_specs=pl.BlockSpec((tc, d), lambda i: (i, 0)),
        out_shape=jax.ShapeDtypeStruct((n, d), F32),
        scratch_shapes=[pltpu.VMEM((TOP_K, tc * PACK_ROWS, V7X_LANES), jnp.uint32), pltpu.SemaphoreType.DMA(())],
        compiler_params=_params("arbitrary"),
        name="moe_combine",
    )(idx, yb, h)


def moe_layer(h, norm_g, rg_w, rg_b, re_w, re_b, wg, wu, wd, layer, bm=256):
    n, d = h.shape
    n_exp = wg.shape[1]
    t, route = moe_router(h, norm_g, rg_w, rg_b, re_w, re_b)
    flat_e = jnp.transpose(route[:TOP_K]).astype(jnp.int32).reshape(-1)
    gate = jnp.transpose(route[TOP_K:2 * TOP_K])

    n_assign = n * TOP_K
    n_pad = n_assign + n_exp * bm
    order = jnp.argsort(flat_e).astype(jnp.int32)
    rank = jnp.argsort(order).astype(jnp.int32)
    counts = jnp.sum(flat_e[None, :] == jnp.arange(n_exp, dtype=jnp.int32)[:, None], axis=1, dtype=jnp.int32)
    padded = (counts + bm - 1) // bm * bm
    padded_end = jnp.cumsum(padded)
    padded_start = padded_end - padded
    start = jnp.cumsum(counts) - counts
    pos = (padded_start[flat_e] + rank - start[flat_e]).reshape(n, TOP_K)
    n_blocks = n_pad // bm
    block_start = jnp.arange(n_blocks, dtype=jnp.int32) * bm
    block_e = jnp.sum(block_start[:, None] >= padded_end[None, :], axis=1, dtype=jnp.int32)
    block_e = jnp.minimum(block_e, n_exp - 1)
    n_used = (padded_end[-1] // bm).astype(jnp.int32).reshape(1)
    slot_e = jnp.repeat(block_e, bm)
    off = jnp.arange(n_pad, dtype=jnp.int32) - padded_start[slot_e]
    valid = off < counts[slot_e]
    src = order[jnp.clip(start[slot_e] + off, 0, n_assign - 1)]
    buf_tok = jnp.where(valid, src // TOP_K, 0)
    buf_w = jnp.where(valid, gate.reshape(-1)[src], 0.0)

    xb = gather_rows(t, buf_tok, bm)
    yb = moe_experts(xb, block_e, n_used, buf_w, wg, wu, wd, layer, bm)
    return moe_combine(h, yb, pos)


def kernel(x, rwkv_norm_g, rwkv_mix, rwkv_w_r, rwkv_w_k, rwkv_w_v, rwkv_w0, rwkv_w1, rwkv_w2, rwkv_a0, rwkv_a1,
           rwkv_a2, rwkv_g1, rwkv_g2, rwkv_k_k, rwkv_k_a, rwkv_r_k, rwkv_gn_g, rwkv_gn_b, rwkv_w_o, kv_norm_g,
           w_kv, k_norm_g, attn_norm_g, attn_w_q, q_norm_g, lambda_q1, lambda_k1, lambda_q2, lambda_k2, subln_g,
           attn_w_o, moe_norm_g, router_group_w, router_group_b, router_expert_w, router_expert_b, expert_w_gate,
           expert_w_up, expert_w_down):
    batch, seq, d = x.shape
    depth = moe_norm_g.shape[0]
    n_a = rwkv_norm_g.shape[0]
    h = x.reshape(batch * seq, d)
    k_shared = v_shared = None
    for l in range(depth):
        if l < n_a:
            i = l
            h = rwkv_layer(h, batch, seq, rwkv_norm_g[i], rwkv_mix[i], rwkv_w_r[i], rwkv_w_k[i], rwkv_w_v[i],
                           rwkv_w0[i], rwkv_w1[i], rwkv_w2[i], rwkv_a0[i], rwkv_a1[i], rwkv_a2[i], rwkv_g1[i],
                           rwkv_g2[i], rwkv_k_k[i], rwkv_k_a[i], rwkv_r_k[i], rwkv_gn_g[i], rwkv_gn_b[i],
                           rwkv_w_o[i])
        else:
            j = l - n_a
            if j == 0:
                k_shared, v_shared = shared_kv(h, kv_norm_g, w_kv, k_norm_g)
            h = attn_layer(h, k_shared, v_shared, batch, seq, l, attn_norm_g[j], attn_w_q[j], q_norm_g[j], k_norm_g,
                           lambda_q1[j], lambda_k1[j], lambda_q2[j], lambda_k2[j], subln_g[j], attn_w_o[j])
        h = moe_layer(h, moe_norm_g[l], router_group_w[l], router_group_b[l], router_expert_w[l],
                      router_expert_b[l], expert_w_gate, expert_w_up, expert_w_down, l)
    return h.reshape(batch, seq, d)
```

```python
import functools
import math

import jax
import jax.numpy as jnp
from jax import lax
from jax.experimental import pallas as pl
from jax.experimental.pallas import tpu as pltpu

V7X_LANES = 128
V7X_SUBLANES = 8
V7X_VMEM_BYTES = 64 * 1024 * 1024
VMEM_LIMIT_BYTES = V7X_VMEM_BYTES - 8 * 1024 * 1024

RWKV_HEAD = 64
DIFF_HEAD = 64
TOP_K = 2
GN_EPS = 64e-5
RMS_EPS = 1e-6
WKV_CHUNK = 64
WKV_PAIRS = 16

BF16 = jnp.bfloat16
F32 = jnp.float32


def _tile(n, pref):
    t = min(n, pref)
    assert n % t == 0, (n, pref)
    return t


def _params(*sem):
    return pltpu.CompilerParams(dimension_semantics=sem, vmem_limit_bytes=VMEM_LIMIT_BYTES)


def _mm(a, b):
    return jnp.dot(a.astype(BF16), b.astype(BF16), preferred_element_type=F32)


def _mm_nt(a, b):
    return lax.dot_general(a.astype(BF16), b.astype(BF16), (((1,), (1,)), ((), ())), preferred_element_type=F32)


def _split_dot(x, w_bf16):
    hi = x.astype(BF16)
    lo = (x - hi.astype(F32)).astype(BF16)
    return (jnp.dot(hi, w_bf16, preferred_element_type=F32) + jnp.dot(lo, w_bf16, preferred_element_type=F32))


def _seg_ones(n, seg):
    i = lax.broadcasted_iota(jnp.int32, (n, n), 0) // seg
    j = lax.broadcasted_iota(jnp.int32, (n, n), 1) // seg
    return jnp.where(i == j, 1.0, 0.0).astype(BF16)


def _rms(x, g):
    return x * lax.rsqrt(jnp.mean(x * x, axis=-1, keepdims=True) + RMS_EPS) * g


def _rmsnorm_kernel(x_ref, g_ref, o_ref):
    o_ref[...] = _rms(x_ref[...], g_ref[...]).astype(o_ref.dtype)


def rmsnorm(x, g, out_dtype=BF16):
    n, d = x.shape
    tm = _tile(n, 512)
    return pl.pallas_call(
        _rmsnorm_kernel,
        grid=(n // tm,),
        in_specs=[pl.BlockSpec((tm, d), lambda i: (i, 0)), pl.BlockSpec((1, d), lambda i: (0, 0))],
        out_specs=pl.BlockSpec((tm, d), lambda i: (i, 0)),
        out_shape=jax.ShapeDtypeStruct((n, d), out_dtype),
        compiler_params=_params("parallel"),
        name="rmsnorm",
    )(x, g.reshape(1, d))


def _matmul_kernel(x_ref, w_ref, *rest, epilogue, n_extra):
    extra = [r[...] for r in rest[:n_extra]]
    o_ref = rest[n_extra]
    acc = jnp.dot(x_ref[...], w_ref[...], preferred_element_type=F32)
    o_ref[...] = epilogue(acc, *extra).astype(o_ref.dtype)


def matmul(x, w, *, out_dtype, epilogue=None, rows=(), tiles=(), w_cols=None, tm=1024, tn=1024, name="matmul"):
    m, k = x.shape
    col0, n_out = (0, w.shape[1]) if w_cols is None else w_cols
    tm = _tile(m, tm)
    tn = _tile(n_out, tn)
    assert col0 % tn == 0
    j0 = col0 // tn
    if epilogue is None:
        epilogue = lambda acc: acc
    in_specs = [pl.BlockSpec((tm, k), lambda i, j: (i, 0)), pl.BlockSpec((k, tn), lambda i, j: (0, j + j0))]
    in_specs += [pl.BlockSpec((1, tn), lambda i, j: (0, j)) for _ in rows]
    in_specs += [pl.BlockSpec((tm, tn), lambda i, j: (i, j)) for _ in tiles]
    return pl.pallas_call(
        functools.partial(_matmul_kernel, epilogue=epilogue, n_extra=len(rows) + len(tiles)),
        grid=(m // tm, n_out // tn),
        in_specs=in_specs,
        out_specs=pl.BlockSpec((tm, tn), lambda i, j: (i, j)),
        out_shape=jax.ShapeDtypeStruct((m, n_out), out_dtype),
        compiler_params=_params("parallel", "arbitrary"),
        name=name,
    )(x, w, *[r.reshape(1, n_out).astype(F32) for r in rows], *tiles)


def _seg_rms_epilogue(acc, g, *, seg, scale):
    ones = _seg_ones(V7X_LANES, seg)
    outs = []
    for s in range(acc.shape[1] // V7X_LANES):
        a = acc[:, s * V7X_LANES:(s + 1) * V7X_LANES]
        ss = _split_dot(a * a, ones)
        outs.append(a * lax.rsqrt(ss * (1.0 / seg) + RMS_EPS))
    return jnp.concatenate(outs, axis=1) * (g * scale)


def _wkv_kernel(r_ref, lw_ref, k_ref, v_ref, a_ref, g_ref, kk_ref, ka_ref, rk_ref, gg_ref, gb_ref, o_ref, h_ref,
                *, chunk, pairs):
    c2 = 2 * chunk
    half = V7X_LANES // 2
    assert c2 == V7X_LANES, "two stacked heads of one chunk must fill one 128-row tile"

    @pl.when(pl.program_id(2) == 0)
    def _():
        h_ref[...] = jnp.zeros_like(h_ref)

    lane = lax.broadcasted_iota(jnp.int32, (chunk, V7X_LANES), 1)
    head0 = lane < half
    ri = lax.broadcasted_iota(jnp.int32, (c2, c2), 0)
    ci = lax.broadcasted_iota(jnp.int32, (c2, c2), 1)
    same = (ri // chunk) == (ci // chunk)
    strict = jnp.logical_and(same, (ci % chunk) < (ri % chunk))
    incl = jnp.logical_and(same, (ci % chunk) <= (ri % chunk))
    eye = jnp.where(ri == ci, 1.0, 0.0)
    li = lax.broadcasted_iota(jnp.int32, (chunk, chunk), 0)
    lj = lax.broadcasted_iota(jnp.int32, (chunk, chunk), 1)
    ltri = jnp.where(li >= lj, 1.0, 0.0).astype(BF16)
    seg = _seg_ones(V7X_LANES, RWKV_HEAD)
    n_double = int(math.log2(chunk)) - 1

    def stack(x):
        return jnp.concatenate([jnp.where(head0, x, 0.0), jnp.where(head0, 0.0, x)], axis=0)

    def hi_lo(x, axis):
        hi = x.astype(BF16)
        return jnp.concatenate([hi, (x - hi.astype(F32)).astype(BF16)], axis=axis)

    def seg_sums(xs):
        prods = [jnp.dot(hi_lo(x, 0), seg, preferred_element_type=F32) for x in xs]
        return [o[:chunk] + o[chunk:] for o in prods]

    lanes = V7X_LANES
    P = range(pairs)
    cols = [slice(p * lanes, (p + 1) * lanes) for p in P]
    zeros = jnp.zeros((c2, lanes), F32)

    def body(c, carry):
        rows = pl.ds(pl.multiple_of(c * chunk, chunk), chunk)
        ld = lambda ref: [ref[rows, cs].astype(F32) for cs in cols]
        par = lambda ref: [ref[:, cs] for cs in cols]
        r, lw, k, v, a = ld(r_ref), ld(lw_ref), ld(k_ref), ld(v_ref), ld(a_ref)
        k_k, k_a, r_k = par(kk_ref), par(ka_ref), par(rk_ref)

        kk = [k[p] * k_k[p] for p in P]
        ss = seg_sums([x * x for x in kk])
        kk = [kk[p] * lax.rsqrt(jnp.maximum(ss[p], 1e-24)) for p in P]
        k = [k[p] * (1.0 + (a[p] - 1.0) * k_a[p]) for p in P]
        b = [kk[p] * a[p] for p in P]
        bonus = seg_sums([r[p] * k[p] * r_k[p] for p in P])

        cc = [jnp.dot(ltri, hi_lo(lw[p], 1), preferred_element_type=F32) for p in P]
        cum = [x[:, :lanes] + x[:, lanes:] for x in cc]
        mid = [x[chunk // 2 - 1:chunk // 2] for x in cum]
        last = [x[chunk - 1:chunk] for x in cum]
        dec_mid = [jnp.exp(mid[p] - cum[p]) for p in P]
        dec_last = [jnp.exp(last[p] - cum[p]) for p in P]
        emid = [jnp.exp(x) for x in mid]
        plast = [jnp.exp(x) for x in last]
        As = [stack(-kk[p] * jnp.exp(cum[p] - lw[p] - mid[p])) for p in P]
        Rs = [stack(r[p] * jnp.exp(cum[p] - mid[p])) for p in P]
        Vs = [stack(v[p]) for p in P]
        AR = [jnp.concatenate([As[p], Rs[p]], axis=0) for p in P]
        BK = [jnp.concatenate([stack(b[p] * dec_mid[p]), stack(k[p] * dec_mid[p])], axis=0) for p in P]
        BKh = [jnp.concatenate([stack(b[p] * dec_last[p]), stack(k[p] * dec_last[p])], axis=0) for p in P]

        G = [_mm_nt(AR[p], BK[p]) for p in P]
        Aab = [jnp.where(strict, g[:c2, :c2], 0.0) for g in G]
        Aak = [jnp.where(strict, g[:c2, c2:], 0.0) for g in G]
        S = [jnp.concatenate([jnp.where(incl, g[c2:, :c2], 0.0), jnp.where(incl, g[c2:, c2:], 0.0)], axis=1)
             for g in G]

        X = [eye + x for x in Aab]
        Q = [_mm(x, x) for x in Aab]
        for _ in range(n_double - 1):
            QX = [_mm(Q[p], jnp.concatenate([Q[p], X[p]], axis=1)) for p in P]
            X = [X[p] + QX[p][:, lanes:] for p in P]
            Q = [x[:, :lanes] for x in QX]
        X = [X[p] + _mm(Q[p], X[p]) for p in P]

        AV = [_mm(Aak[p], Vs[p]) for p in P]
        TA = [_mm(X[p], jnp.concatenate([As[p], AV[p]], axis=1)) for p in P]
        lowV = [jnp.concatenate([zeros, Vs[p]], axis=1) for p in P]
        SY = [_mm(S[p], jnp.concatenate([TA[p], lowV[p]], axis=0)) for p in P]
        Rbar = [(Rs[p] + SY[p][:, :lanes]) * emid[p] for p in P]
        AU = [jnp.concatenate([TA[p][:, :lanes] * emid[p], TA[p][:, lanes:]], axis=1) for p in P]
        BKT = [jnp.transpose(x) for x in BKh]
        MN = [_mm(BKT[p], jnp.concatenate([AU[p], lowV[p]], axis=0)) for p in P]
        M = [eye * plast[p] + MN[p][:, :lanes] for p in P]
        H = [h_ref[p] for p in P]
        RH = [_mm(jnp.concatenate([Rbar[p], M[p]], axis=0), H[p]) for p in P]
        for p in P:
            h_ref[p] = RH[p][c2:] + MN[p][:, lanes:]
        Ys = [RH[p][:c2] + SY[p][:, lanes:] for p in P]
        y = [x[:chunk] + x[chunk:] for x in Ys]

        inv = 1.0 / RWKV_HEAD
        mean = seg_sums(y)
        yc = [y[p] - mean[p] * inv for p in P]
        var = seg_sums([x * x for x in yc])
        gn_g, gn_b = par(gg_ref), par(gb_ref)
        for p in P:
            yn = yc[p] * lax.rsqrt(var[p] * inv + GN_EPS) * gn_g[p] + gn_b[p]
            gate = g_ref[rows, cols[p]].astype(F32)
            o_ref[rows, cols[p]] = ((yn + bonus[p] * v[p]) * gate).astype(o_ref.dtype)
        return carry

    lax.fori_loop(0, r_ref.shape[0] // chunk, body, 0)


def wkv7(rkv, lw, a, g, k_k, k_a, r_k, gn_g, gn_b, batch, seq):
    n, d = lw.shape
    pairs = min(WKV_PAIRS, d // V7X_LANES)
    width = pairs * V7X_LANES
    groups = d // width
    tb = _tile(seq, 256)
    seq_spec = lambda part: pl.BlockSpec((tb, width), lambda b, p, t: (b * (seq // tb) + t, p + part * groups))
    par_spec = pl.BlockSpec((1, width), lambda b, p, t: (0, p))
    row = lambda x: x.reshape(1, d).astype(F32)
    return pl.pallas_call(
        functools.partial(_wkv_kernel, chunk=WKV_CHUNK, pairs=pairs),
        grid=(batch, groups, seq // tb),
        in_specs=[seq_spec(0), seq_spec(0), seq_spec(1), seq_spec(2), seq_spec(0), seq_spec(0)] + [par_spec] * 5,
        out_specs=seq_spec(0),
        out_shape=jax.ShapeDtypeStruct((n, d), BF16),
        scratch_shapes=[pltpu.VMEM((pairs, V7X_LANES, V7X_LANES), F32)],
        compiler_params=_params("parallel", "parallel", "arbitrary"),
        name="wkv7",
    )(rkv, lw, rkv, rkv, a, g, row(k_k), row(k_a), row(r_k), row(gn_g), row(gn_b))


def _decay_epilogue(acc, w0):
    z = -(w0 + acc)
    softplus = jnp.maximum(z, 0.0) + jnp.log(1.0 + jnp.exp(-jnp.abs(z)))
    return -jnp.exp(-softplus - 0.5)


def _rwkv_proj_kernel(x_ref, prev_ref, g_ref, mix_ref, wrkv_ref, w1_ref, a1_ref, g1_ref, w2_ref, a2_ref, g2_ref,
                      w0_ref, a0_ref, rkv_ref, lw_ref, a_ref, gate_ref, xs_ref, *, tm, seq, tiles_per_proj):
    i = pl.program_id(0)
    j = pl.program_id(1)

    @pl.when(j == 0)
    def _():
        g = g_ref[...]
        xn = _rms(x_ref[...], g)
        prev = _rms(prev_ref[...], g)[V7X_SUBLANES - 1:V7X_SUBLANES]
        prev = jnp.where((i * tm) % seq == 0, 0.0, prev)
        row = lax.broadcasted_iota(jnp.int32, xn.shape, 0)
        xx = jnp.where(row == 0, prev, pltpu.roll(xn, 1, 0)) - xn
        mixed = lambda s: (xn + xx * mix_ref[s:s + 1, :]).astype(BF16)
        for slot, s in enumerate((0, 2, 3)):
            xs_ref[slot] = mixed(s)
        dot = lambda a, w_ref: jnp.dot(a, w_ref[...], preferred_element_type=F32)
        tw = jnp.tanh(dot(mixed(1), w1_ref)).astype(BF16)
        lw_ref[...] = _decay_epilogue(dot(tw, w2_ref), w0_ref[...])
        ta = dot(mixed(4), a1_ref).astype(BF16)
        a_ref[...] = jax.nn.sigmoid(a0_ref[...] + dot(ta, a2_ref)).astype(a_ref.dtype)
        tg = jax.nn.sigmoid(dot(mixed(5), g1_ref)).astype(BF16)
        gate_ref[...] = dot(tg, g2_ref).astype(gate_ref.dtype)

    x_sel = xs_ref[j // tiles_per_proj]
    rkv_ref[...] = jnp.dot(x_sel, wrkv_ref[...], preferred_element_type=F32).astype(rkv_ref.dtype)


def rwkv_projections(h, norm_g, mix, w_r, w_k, w_v, w0, w1, w2, a0, a1, a2, g1, g2, seq):
    n, d = h.shape
    tm = _tile(seq, 256)
    tn = _tile(d, 512)
    per = tm // V7X_SUBLANES
    bf = lambda w: w.astype(BF16)
    pad_cols = lambda w: jnp.pad(w, ((0, 0), (0, -w.shape[1] % V7X_LANES)))
    pad_rows = lambda w: jnp.pad(w, ((0, -w.shape[0] % V7X_LANES), (0, 0)))
    consts = [norm_g.reshape(1, d), mix]
    weights = [bf(pad_cols(w1)), bf(pad_cols(a1)), bf(pad_cols(g1)), bf(pad_rows(w2)), bf(pad_rows(a2)),
               bf(pad_rows(g2)), w0.reshape(1, d), a0.reshape(1, d)]
    whole = lambda arr: pl.BlockSpec(arr.shape, lambda i, j: (0, 0))
    rows = pl.BlockSpec((tm, d), lambda i, j: (i, 0))
    return pl.pallas_call(
        functools.partial(_rwkv_proj_kernel, tm=tm, seq=seq, tiles_per_proj=d // tn),
        grid=(n // tm, 3 * d // tn),
        in_specs=[rows, pl.BlockSpec((V7X_SUBLANES, d), lambda i, j: (jnp.maximum(i * per - 1, 0), 0))]
        + [whole(c) for c in consts] + [pl.BlockSpec((d, tn), lambda i, j: (0, j))] + [whole(w) for w in weights],
        out_specs=[pl.BlockSpec((tm, tn), lambda i, j: (i, j)), rows, rows, rows],
        out_shape=[jax.ShapeDtypeStruct((n, 3 * d), BF16), jax.ShapeDtypeStruct((n, d), F32),
                   jax.ShapeDtypeStruct((n, d), BF16), jax.ShapeDtypeStruct((n, d), BF16)],
        scratch_shapes=[pltpu.VMEM((3, tm, d), BF16)],
        compiler_params=_params("parallel", "arbitrary"),
        name="rwkv_proj",
    )(h, h, *consts, bf(jnp.concatenate([w_r, w_k, w_v], axis=1)), *weights)


def rwkv_layer(h, batch, seq, norm_g, mix, w_r, w_k, w_v, w0, w1, w2, a0, a1, a2, g1, g2, k_k, k_a, r_k, gn_g,
               gn_b, w_o):
    rkv, lw, a, g = rwkv_projections(h, norm_g, mix, w_r, w_k, w_v, w0, w1, w2, a0, a1, a2, g1, g2, seq)
    y = wkv7(rkv, lw, a, g, k_k, k_a, r_k, gn_g, gn_b, batch, seq)
    return matmul(y, w_o.astype(BF16), out_dtype=F32, tiles=(h,), epilogue=lambda acc, res: res + acc,
                  name="rwkv_o")


LOG2_E = 1.4426950408889634
ATTN_UNSHIFTED_MAX_LOG2 = 80.0
ATTN_UNROLL = 4
ATTN_LAG = 1
ATTN_SUM_ROWS = 16


def _sub_head_queries(q):
    lane = lax.broadcasted_iota(jnp.int32, q.shape, 1)
    zero = jnp.zeros_like(q)
    return jnp.where(lane < DIFF_HEAD, q, zero), jnp.where(lane < DIFF_HEAD, zero, q)


def _causal_tile(tq):
    return lax.broadcasted_iota(jnp.int32, (tq, tq), 0) >= lax.broadcasted_iota(jnp.int32, (tq, tq), 1)


def _attn_online_kernel(lam_ref, q_ref, k_ref, v_ref, sg_ref, o_ref, *, tq, out_scale):
    qi = pl.program_id(2)
    qs = _sub_head_queries(q_ref[...])
    causal = _causal_tile(tq)

    def step(j, carry, masked):
        start = pl.multiple_of(j * tq, tq)
        kj = k_ref[pl.ds(start, tq), :]
        vj = v_ref[pl.ds(start, tq), :]
        out = []
        for sub in range(2):
            m, l, acc = carry[sub]
            s = lax.dot_general(qs[sub], kj, (((1,), (1,)), ((), ())), preferred_element_type=F32)
            if masked:
                s = jnp.where(causal, s, -1e30)
            m_new = jnp.maximum(m, jnp.max(s, axis=-1, keepdims=True))
            alpha = jnp.exp2(m - m_new)
            p = jnp.exp2(s - m_new)
            l = alpha * l + jnp.sum(p, axis=-1, keepdims=True)
            acc = alpha * acc + jnp.dot(p.astype(BF16), vj, preferred_element_type=F32)
            out.append((m_new, l, acc))
        return tuple(out)

    init = tuple((jnp.full((tq, 1), -1e30, F32), jnp.zeros((tq, 1), F32), jnp.zeros((tq, V7X_LANES), F32))
                 for _ in range(2))
    carry = lax.fori_loop(0, qi, lambda j, c: step(j, c, False), init)
    (_, l0, acc0), (_, l1, acc1) = step(qi, carry, True)
    o = acc0 / l0 - lam_ref[0] * (acc1 / l1)
    o_ref[...] = (_rms(o, sg_ref[...]) * out_scale).astype(o_ref.dtype)


def _attn_unshifted_kernel(lam_ref, q_ref, k_ref, vt_ref, sg_ref, o_ref, acc_ref, *, tq, out_scale):
    qi = pl.program_id(2)
    qs = _sub_head_queries(q_ref[...])
    visible = lax.broadcasted_iota(jnp.int32, (tq, tq), 1) >= lax.broadcasted_iota(jnp.int32, (tq, tq), 0)
    ones = jnp.ones((ATTN_SUM_ROWS, tq), BF16)
    acc_ref[...] = jnp.zeros_like(acc_ref)

    def scores(j, sub, masked):
        kj = k_ref[pl.ds(pl.multiple_of(j * tq, tq), tq), :]
        st = lax.dot_general(kj, qs[sub], (((1,), (1,)), ((), ())), preferred_element_type=F32)
        return jnp.where(visible, st, -1e30) if masked else st

    def accumulate(j, sub, st):
        vt = jnp.concatenate([vt_ref[j], ones], axis=0)
        acc_ref[sub] += jnp.dot(vt, jnp.exp2(st).astype(BF16), preferred_element_type=F32)

    def steps(tiles):
        work = [(j, sub, masked) for j, masked in tiles for sub in range(2)]
        pending = []
        for j, sub, masked in work:
            pending.append((j, sub, scores(j, sub, masked)))
            if len(pending) > ATTN_LAG:
                accumulate(*pending.pop(0))
        for item in pending:
            accumulate(*item)

    def body(jj, c):
        steps([(jj * ATTN_UNROLL + u, False) for u in range(ATTN_UNROLL)])
        return c

    n_full = qi // ATTN_UNROLL
    lax.fori_loop(0, n_full, body, 0)
    first = n_full * ATTN_UNROLL
    for rem in range(ATTN_UNROLL):
        @pl.when(qi - first == rem)
        def _():
            steps([(first + u, False) for u in range(rem)] + [(qi, True)])
    a0 = acc_ref[0]
    a1 = acc_ref[1]
    hw = 2 * DIFF_HEAD
    ot = a0[:hw] / a0[hw:hw + 1] - lam_ref[0] * (a1[:hw] / a1[hw:hw + 1])
    ot = ot * lax.rsqrt(jnp.mean(ot * ot, axis=0, keepdims=True) + RMS_EPS)
    o_ref[...] = (jnp.transpose(ot) * (sg_ref[...] * out_scale)).astype(o_ref.dtype)


def diff_attention_core(q, k, v, lam, subln_g, batch, seq, lam_init, unshifted):
    n, d = q.shape
    tq = _tile(seq, 512)
    nq = seq // tq
    head_w = 2 * DIFF_HEAD
    q_spec = pl.BlockSpec((tq, head_w), lambda b, h, i: (b * nq + i, h))
    kv_spec = pl.BlockSpec((seq, head_w), lambda b, h, i: (b, h))
    if unshifted:
        body = _attn_unshifted_kernel
        scratch = [pltpu.VMEM((2, head_w + ATTN_SUM_ROWS, tq), F32)]
        v = jnp.transpose(v.reshape(n // tq, tq, d), (0, 2, 1))
        v_spec = pl.BlockSpec((nq, head_w, tq), lambda b, h, i: (b, h, 0))
    else:
        body = _attn_online_kernel
        scratch = []
        v_spec = kv_spec
    return pl.pallas_call(
        functools.partial(body, tq=tq, out_scale=1.0 - lam_init),
        grid=(batch, d // head_w, nq),
        in_specs=[pl.BlockSpec(memory_space=pltpu.SMEM), q_spec, kv_spec, v_spec,
                  pl.BlockSpec((1, head_w), lambda b, h, i: (0, 0))],
        out_specs=q_spec,
        out_shape=jax.ShapeDtypeStruct((n, d), BF16),
        scratch_shapes=scratch,
        compiler_params=_params("parallel", "parallel", "arbitrary"),
        name="diff_attn_unshifted" if unshifted else "diff_attn_online",
    )(lam.reshape(1).astype(F32), q, k, v, subln_g.reshape(1, head_w).astype(F32))


def shared_kv(h, kv_norm_g, w_kv, k_norm_g):
    d = h.shape[1]
    hn = rmsnorm(h, kv_norm_g)
    kg = jnp.tile(k_norm_g, d // DIFF_HEAD)
    w = w_kv.astype(BF16)
    k = matmul(hn, w, w_cols=(0, d), out_dtype=BF16, rows=(kg,), name="kv_k", tn=512,
               epilogue=functools.partial(_seg_rms_epilogue, seg=DIFF_HEAD, scale=1.0))
    v = matmul(hn, w, w_cols=(d, d), out_dtype=BF16, name="kv_v")
    return k, v


def attn_layer(h, k, v, batch, seq, layer_idx, norm_g, w_q, q_norm_g, k_norm_g, lq1, lk1, lq2, lk2, subln_g, w_o):
    d = h.shape[1]
    hn = rmsnorm(h, norm_g)
    qg = jnp.tile(q_norm_g, d // DIFF_HEAD)
    q_scale = LOG2_E * DIFF_HEAD ** -0.5
    q = matmul(hn, w_q.astype(BF16), out_dtype=BF16, rows=(qg,), name="attn_q", tn=512,
               epilogue=functools.partial(_seg_rms_epilogue, seg=DIFF_HEAD, scale=q_scale))
    lam_init = 0.8 - 0.6 * math.exp(-0.3 * layer_idx)
    lam = jnp.exp(jnp.sum(lq1 * lk1)) - jnp.exp(jnp.sum(lq2 * lk2)) + lam_init
    bound = 1.01 * q_scale * DIFF_HEAD * jnp.max(jnp.abs(q_norm_g)) * jnp.max(jnp.abs(k_norm_g))
    core = lambda unshifted: functools.partial(diff_attention_core, batch=batch, seq=seq, lam_init=lam_init,
                                               unshifted=unshifted)
    o = lax.cond(bound <= ATTN_UNSHIFTED_MAX_LOG2, core(True), core(False), q, k, v, lam, subln_g)
    return matmul(o, w_o.astype(BF16), out_dtype=F32, tiles=(h,), epilogue=lambda acc, res: res + acc,
                  name="attn_o")


PACK_ROWS = V7X_SUBLANES
ROUTE_ROWS = V7X_SUBLANES
DMA_LOOP_UNROLL = 8
DMA_PRIORITIES = 2
EXPERT_WEIGHT_CHUNK_BYTES = 1 << 20
EXPERT_STAGE_SLOTS = 4


def _pack_store(x, o_ref):
    m = x.shape[0]
    assert x.shape[1] == 2 * PACK_ROWS * V7X_LANES
    word = lambda j: pltpu.bitcast(x[:, j * V7X_LANES:(j + 1) * V7X_LANES].astype(BF16).astype(F32), jnp.uint32)
    for s in range(PACK_ROWS):
        o_ref[pl.ds(s, m, stride=PACK_ROWS), :] = word(s) | (word(s + PACK_ROWS) >> 16)


def _unpack_load(p_ref, m):
    words = [p_ref[pl.ds(s, m, stride=PACK_ROWS), :] for s in range(PACK_ROWS)]
    hi = [pltpu.bitcast(w & jnp.uint32(0xFFFF0000), F32) for w in words]
    lo = [pltpu.bitcast(w << 16, F32) for w in words]
    return jnp.concatenate(hi + lo, axis=1)


def _router_kernel(x_ref, g_ref, w_ref, b_ref, t_ref, route_ref, *, n_groups, per_group):
    t = _rms(x_ref[...], g_ref[...])
    _pack_store(t, t_ref)
    logits = jnp.dot(t, w_ref[...], preferred_element_type=F32, precision=lax.Precision.HIGHEST) + b_ref[...]
    lane = lax.broadcasted_iota(jnp.int32, logits.shape, 1)
    none = jnp.int32(logits.shape[1])
    neg = -jnp.inf
    rmax = lambda x: jnp.max(x, axis=-1, keepdims=True)
    first = lambda hit: jnp.min(jnp.where(hit, lane, none), axis=-1, keepdims=True)

    is_group = lane < n_groups
    g_logit = jnp.where(is_group, logits, neg)
    g_max = rmax(g_logit)
    grp = first(g_logit == g_max)
    p_group = 1.0 / jnp.sum(jnp.where(is_group, jnp.exp(logits - g_max), 0.0), axis=-1, keepdims=True)

    lo = n_groups + grp * per_group
    e_logit = jnp.where(jnp.logical_and(lane >= lo, lane < lo + per_group), logits, neg)
    top1 = rmax(e_logit)
    i1 = first(e_logit == top1)
    e_rest = jnp.where(lane == i1, neg, e_logit)
    top2 = rmax(e_rest)
    i2 = first(e_rest == top2)
    ratio = jnp.exp(top2 - top1)
    gate1 = p_group / (1.0 + ratio)
    gate2 = gate1 * ratio

    cols = [(i1 - n_groups).astype(F32), (i2 - n_groups).astype(F32), gate1, gate2]
    out = jnp.zeros_like(logits)
    for c, val in enumerate(cols):
        out = jnp.where(lane == c, val, out)
    route_ref[...] = jnp.transpose(out)[:ROUTE_ROWS]


def moe_router(h, norm_g, rg_w, rg_b, re_w, re_b):
    n, d = h.shape
    tm = _tile(n, 256)
    n_groups, n_exp = rg_w.shape[1], re_w.shape[1]
    assert n_groups + n_exp <= V7X_LANES
    pad = V7X_LANES - n_groups - n_exp
    w = jnp.pad(jnp.concatenate([rg_w, re_w], axis=1), ((0, 0), (0, pad)))
    b = jnp.pad(jnp.concatenate([rg_b, re_b]), (0, pad)).reshape(1, -1)
    return pl.pallas_call(
        functools.partial(_router_kernel, n_groups=n_groups, per_group=n_exp // n_groups),
        grid=(n // tm,),
        in_specs=[pl.BlockSpec((tm, d), lambda i: (i, 0)), pl.BlockSpec((1, d), lambda i: (0, 0)),
                  pl.BlockSpec((d, V7X_LANES), lambda i: (0, 0)), pl.BlockSpec((1, V7X_LANES), lambda i: (0, 0))],
        out_specs=[pl.BlockSpec((tm * PACK_ROWS, V7X_LANES), lambda i: (i, 0)),
                   pl.BlockSpec((ROUTE_ROWS, tm), lambda i: (0, i))],
        out_shape=[jax.ShapeDtypeStruct((n * PACK_ROWS, V7X_LANES), jnp.uint32),
                   jax.ShapeDtypeStruct((ROUTE_ROWS, n), F32)],
        compiler_params=_params("parallel"),
        name="moe_router",
    )(h, norm_g.reshape(1, d), w, b)


def _token_copy(src_hbm, token, dst, slot, sem):
    src = src_hbm.at[pl.ds(pl.multiple_of(token * PACK_ROWS, PACK_ROWS), PACK_ROWS), :]
    return pltpu.make_async_copy(src, dst.at[pl.ds(pl.multiple_of(slot * PACK_ROWS, PACK_ROWS), PACK_ROWS), :], sem)


def _expert_kernel(be_ref, nused_ref, idx_ref, idx_next_ref, x_hbm, wg_hbm, wu_hbm, wd_hbm, rw_ref, o_ref,
                   xbuf, wg_v, wu_v, wd_v, stage_in, stage_out, xsem, wsem, *, bm, layer, n_blocks):
    i = pl.program_id(0)
    slot = i % 2
    e = be_ref[i]
    used = i < nused_ref[0]
    new_expert = jnp.logical_or(i == 0, e != be_ref[jnp.maximum(i - 1, 0)])

    def gather(idx, s):
        for r in range(bm):
            _token_copy(x_hbm, idx[0, 0, r], xbuf.at[s], r, xsem.at[s]).start(priority=r % DMA_PRIORITIES)

    def drain(s):
        def body(r, c):
            _token_copy(x_hbm, 0, xbuf.at[s], r, xsem.at[s]).wait()
            return c
        lax.fori_loop(0, bm, body, 0, unroll=DMA_LOOP_UNROLL)

    def load(w_hbm, w_v, stage):
        slots, chunk = stage.shape[0], stage.shape[1]
        n_chunks = w_v.shape[0] // chunk
        copy = lambda c: pltpu.make_async_copy(w_hbm.at[layer, e, pl.ds(c * chunk, chunk), :], stage.at[c % slots],
                                               wsem.at[c % slots])
        for c in range(min(slots - 1, n_chunks)):
            copy(c).start()
        for c in range(n_chunks):
            if c + slots - 1 < n_chunks:
                copy(c + slots - 1).start()
            copy(c).wait()
            w_v[pl.ds(c * chunk, chunk), :] = stage[c % slots].astype(BF16)

    @pl.when(i == 0)
    def _():
        gather(idx_ref, 0)

    drain(slot)

    @pl.when(jnp.logical_and(used, new_expert))
    def _():
        load(wg_hbm, wg_v, stage_in)
        load(wu_hbm, wu_v, stage_in)
        load(wd_hbm, wd_v, stage_out)

    @pl.when(used)
    def _():
        gather(idx_next_ref, 1 - slot)
        x = _unpack_load(xbuf.at[slot], bm).astype(BF16)
        hg = jnp.dot(x, wg_v[...], preferred_element_type=F32)
        hu = jnp.dot(x, wu_v[...], preferred_element_type=F32)
        act = (hg * jax.nn.sigmoid(hg) * hu).astype(BF16)
        _pack_store(jnp.dot(act, wd_v[...], preferred_element_type=F32) * rw_ref[...], o_ref)

    @pl.when(jnp.logical_not(used))
    def _():
        gather(idx_next_ref, 1 - slot)
        o_ref[...] = jnp.zeros_like(o_ref)

    @pl.when(i == n_blocks - 1)
    def _():
        drain(1 - slot)


def moe_experts(x_packed, buf_tok, block_e, n_used, row_w, wg, wu, wd, layer, bm):
    n_pad = row_w.shape[0]
    n_blocks = n_pad // bm
    d, hid = wg.shape[2], wg.shape[3]
    chunk_in = EXPERT_WEIGHT_CHUNK_BYTES // (hid * 4)
    chunk_out = EXPERT_WEIGHT_CHUNK_BYTES // (d * 4)
    assert d % chunk_in == 0 and hid % chunk_out == 0
    packed = pl.BlockSpec((bm * PACK_ROWS, V7X_LANES), lambda i, be, nu: (i, 0))
    hbm = pl.BlockSpec(memory_space=pl.ANY)
    idx = buf_tok.reshape(n_blocks, 1, bm)
    return pl.pallas_call(
        functools.partial(_expert_kernel, bm=bm, layer=layer, n_blocks=n_blocks),
        grid_spec=pltpu.PrefetchScalarGridSpec(
            num_scalar_prefetch=2,
            grid=(n_blocks,),
            in_specs=[
                pl.BlockSpec((1, 1, bm), lambda i, be, nu: (i, 0, 0), memory_space=pltpu.SMEM),
                pl.BlockSpec((1, 1, bm), lambda i, be, nu: (jnp.minimum(i + 1, n_blocks - 1), 0, 0),
                             memory_space=pltpu.SMEM),
                hbm, hbm, hbm, hbm, pl.BlockSpec((bm, 1), lambda i, be, nu: (i, 0))],
            out_specs=packed,
            scratch_shapes=[
                pltpu.VMEM((2, bm * PACK_ROWS, V7X_LANES), jnp.uint32),
                pltpu.VMEM((d, hid), BF16), pltpu.VMEM((d, hid), BF16), pltpu.VMEM((hid, d), BF16),
                pltpu.VMEM((EXPERT_STAGE_SLOTS, chunk_in, hid), F32),
                pltpu.VMEM((EXPERT_STAGE_SLOTS, chunk_out, d), F32),
                pltpu.SemaphoreType.DMA((2,)), pltpu.SemaphoreType.DMA((EXPERT_STAGE_SLOTS,)),
            ],
        ),
        out_shape=jax.ShapeDtypeStruct((n_pad * PACK_ROWS, V7X_LANES), jnp.uint32),
        compiler_params=_params("arbitrary"),
        name="moe_experts",
    )(block_e, n_used, idx, idx, x_packed, wg, wu, wd, row_w.reshape(n_pad, 1))


def _combine_kernel(idx_ref, yb_hbm, h_ref, o_ref, buf, sem, *, rows):
    def issue(r, c):
        for s in range(TOP_K):
            _token_copy(yb_hbm, idx_ref[0, s, r], buf.at[s], r, sem).start(priority=s % DMA_PRIORITIES)
        return c

    def drain(r, c):
        for s in range(TOP_K):
            _token_copy(yb_hbm, 0, buf.at[s], r, sem).wait()
        return c

    lax.fori_loop(0, rows, issue, 0, unroll=DMA_LOOP_UNROLL)
    lax.fori_loop(0, rows, drain, 0, unroll=DMA_LOOP_UNROLL)
    acc = h_ref[...]
    for s in range(TOP_K):
        acc = acc + _unpack_load(buf.at[s], rows)
    o_ref[...] = acc


def moe_combine(h, yb, pos, rows_per_step=256):
    n, d = h.shape
    tc = _tile(n, rows_per_step)
    idx = pos.reshape(n // tc, tc, TOP_K).transpose(0, 2, 1)
    return pl.pallas_call(
        functools.partial(_combine_kernel, rows=tc),
        grid=(n // tc,),
        in_specs=[pl.BlockSpec((1, TOP_K, tc), lambda i: (i, 0, 0), memory_space=pltpu.SMEM),
                  pl.BlockSpec(memory_space=pl.ANY),
                  pl.BlockSpec((tc, d), lambda i: (i, 0))],
        out_specs=pl.BlockSpec((tc, d), lambda i: (i, 0)),
        out_shape=jax.ShapeDtypeStruct((n, d), F32),
        scratch_shapes=[pltpu.VMEM((TOP_K, tc * PACK_ROWS, V7X_LANES), jnp.uint32), pltpu.SemaphoreType.DMA(())],
        compiler_params=_params("arbitrary"),
        name="moe_combine",
    )(idx, yb, h)


def moe_layer(h, norm_g, rg_w, rg_b, re_w, re_b, wg, wu, wd, layer, bm=256):
    n, d = h.shape
    n_exp = wg.shape[1]
    t, route = moe_router(h, norm_g, rg_w, rg_b, re_w, re_b)
    flat_e = jnp.transpose(route[:TOP_K]).astype(jnp.int32).reshape(-1)
    gate = jnp.transpose(route[TOP_K:2 * TOP_K])

    n_assign = n * TOP_K
    n_pad = n_assign + n_exp * bm
    order = jnp.argsort(flat_e).astype(jnp.int32)
    rank = jnp.argsort(order).astype(jnp.int32)
    experts = jnp.arange(n_exp, dtype=jnp.int32)
    counts = jnp.sum(flat_e[None, :] == experts[:, None], axis=1, dtype=jnp.int32)
    padded = (counts + bm - 1) // bm * bm
    padded_end = jnp.cumsum(padded)
    padded_start = padded_end - padded
    start = jnp.cumsum(counts) - counts
    shift = padded_start - start
    pos = (rank + jnp.sum(jnp.where(flat_e[None, :] == experts[:, None], shift[:, None], 0), axis=0)).reshape(n, TOP_K)
    n_blocks = n_pad // bm
    block_start = jnp.arange(n_blocks, dtype=jnp.int32) * bm
    block_e = jnp.sum(block_start[:, None] >= padded_end[None, :], axis=1, dtype=jnp.int32)
    block_e = jnp.minimum(block_e, n_exp - 1)
    n_used = (padded_end[-1] // bm).astype(jnp.int32).reshape(1)
    per_row = lambda table: jnp.repeat(table[block_e], bm)
    off = jnp.arange(n_pad, dtype=jnp.int32) - per_row(padded_start)
    valid = off < per_row(counts)
    src = order[jnp.clip(per_row(start) + off, 0, n_assign - 1)]
    buf_tok = jnp.where(valid, src // TOP_K, 0)
    buf_w = jnp.where(valid, gate.reshape(-1)[src], 0.0)

    yb = moe_experts(t, buf_tok, block_e, n_used, buf_w, wg, wu, wd, layer, bm)
    return moe_combine(h, yb, pos)


def kernel(x, rwkv_norm_g, rwkv_mix, rwkv_w_r, rwkv_w_k, rwkv_w_v, rwkv_w0, rwkv_w1, rwkv_w2, rwkv_a0, rwkv_a1,
           rwkv_a2, rwkv_g1, rwkv_g2, rwkv_k_k, rwkv_k_a, rwkv_r_k, rwkv_gn_g, rwkv_gn_b, rwkv_w_o, kv_norm_g,
           w_kv, k_norm_g, attn_norm_g, attn_w_q, q_norm_g, lambda_q1, lambda_k1, lambda_q2, lambda_k2, subln_g,
           attn_w_o, moe_norm_g, router_group_w, router_group_b, router_expert_w, router_expert_b, expert_w_gate,
           expert_w_up, expert_w_down):
    batch, seq, d = x.shape
    depth = moe_norm_g.shape[0]
    n_a = rwkv_norm_g.shape[0]
    h = x.reshape(batch * seq, d)
    k_shared = v_shared = None
    for l in range(depth):
        if l < n_a:
            i = l
            h = rwkv_layer(h, batch, seq, rwkv_norm_g[i], rwkv_mix[i], rwkv_w_r[i], rwkv_w_k[i], rwkv_w_v[i],
                           rwkv_w0[i], rwkv_w1[i], rwkv_w2[i], rwkv_a0[i], rwkv_a1[i], rwkv_a2[i], rwkv_g1[i],
                           rwkv_g2[i], rwkv_k_k[i], rwkv_k_a[i], rwkv_r_k[i], rwkv_gn_g[i], rwkv_gn_b[i],
                           rwkv_w_o[i])
        else:
            j = l - n_a
            if j == 0:
                k_shared, v_shared = shared_kv(h, kv_norm_g, w_kv, k_norm_g)
            h = attn_layer(h, k_shared, v_shared, batch, seq, l, attn_norm_g[j], attn_w_q[j], q_norm_g[j], k_norm_g,
                           lambda_q1[j], lambda_k1[j], lambda_q2[j], lambda_k2[j], subln_g[j], attn_w_o[j])
        h = moe_layer(h, moe_norm_g[l], router_group_w[l], router_group_b[l], router_expert_w[l],
                      router_expert_b[l], expert_w_gate, expert_w_up, expert_w_down, l)
    return h.reshape(batch, seq, d)
```

```python
import functools
import math

import jax
import jax.numpy as jnp
from jax import lax
from jax.experimental import pallas as pl
from jax.experimental.pallas import tpu as pltpu

V7X_LANES = 128
V7X_SUBLANES = 8
V7X_VMEM_BYTES = 64 * 1024 * 1024
VMEM_LIMIT_BYTES = V7X_VMEM_BYTES - 8 * 1024 * 1024

RWKV_HEAD = 64
DIFF_HEAD = 64
TOP_K = 2
GN_EPS = 64e-5
RMS_EPS = 1e-6
WKV_CHUNK = 64
WKV_PAIRS = 16

BF16 = jnp.bfloat16
F32 = jnp.float32


def _tile(n, pref):
    t = min(n, pref)
    assert n % t == 0, (n, pref)
    return t


def _params(*sem):
    return pltpu.CompilerParams(dimension_semantics=sem, vmem_limit_bytes=VMEM_LIMIT_BYTES)


def _mm(a, b):
    return jnp.dot(a.astype(BF16), b.astype(BF16), preferred_element_type=F32)


def _mm_nt(a, b):
    return lax.dot_general(a.astype(BF16), b.astype(BF16), (((1,), (1,)), ((), ())), preferred_element_type=F32)


def _split_dot(x, w_bf16):
    hi = x.astype(BF16)
    lo = (x - hi.astype(F32)).astype(BF16)
    return (jnp.dot(hi, w_bf16, preferred_element_type=F32) + jnp.dot(lo, w_bf16, preferred_element_type=F32))


def _seg_ones(n, seg):
    i = lax.broadcasted_iota(jnp.int32, (n, n), 0) // seg
    j = lax.broadcasted_iota(jnp.int32, (n, n), 1) // seg
    return jnp.where(i == j, 1.0, 0.0).astype(BF16)


def _rms(x, g):
    return x * lax.rsqrt(jnp.mean(x * x, axis=-1, keepdims=True) + RMS_EPS) * g


def _rmsnorm_kernel(x_ref, g_ref, o_ref):
    o_ref[...] = _rms(x_ref[...], g_ref[...]).astype(o_ref.dtype)


def rmsnorm(x, g, out_dtype=BF16):
    n, d = x.shape
    tm = _tile(n, 512)
    return pl.pallas_call(
        _rmsnorm_kernel,
        grid=(n // tm,),
        in_specs=[pl.BlockSpec((tm, d), lambda i: (i, 0)), pl.BlockSpec((1, d), lambda i: (0, 0))],
        out_specs=pl.BlockSpec((tm, d), lambda i: (i, 0)),
        out_shape=jax.ShapeDtypeStruct((n, d), out_dtype),
        compiler_params=_params("parallel"),
        name="rmsnorm",
    )(x, g.reshape(1, d))


def _matmul_kernel(x_ref, w_ref, *rest, epilogue, n_extra):
    extra = [r[...] for r in rest[:n_extra]]
    o_ref = rest[n_extra]
    acc = jnp.dot(x_ref[...], w_ref[...], preferred_element_type=F32)
    o_ref[...] = epilogue(acc, *extra).astype(o_ref.dtype)


def matmul(x, w, *, out_dtype, epilogue=None, rows=(), tiles=(), x_sel=None, w_cols=None, tm=1024, tn=1024,
           name="matmul"):
    if x_sel is None:
        m, k = x.shape
        x_spec = lambda tm_: pl.BlockSpec((tm_, k), lambda i, j: (i, 0))
    else:
        _, m, k = x.shape
        x_spec = lambda tm_: pl.BlockSpec((None, tm_, k), lambda i, j: (x_sel, i, 0))
    col0, n_out = (0, w.shape[1]) if w_cols is None else w_cols
    tm = _tile(m, tm)
    tn = _tile(n_out, tn)
    assert col0 % tn == 0
    j0 = col0 // tn
    if epilogue is None:
        epilogue = lambda acc: acc
    in_specs = [x_spec(tm), pl.BlockSpec((k, tn), lambda i, j: (0, j + j0))]
    in_specs += [pl.BlockSpec((1, tn), lambda i, j: (0, j)) for _ in rows]
    in_specs += [pl.BlockSpec((tm, tn), lambda i, j: (i, j)) for _ in tiles]
    return pl.pallas_call(
        functools.partial(_matmul_kernel, epilogue=epilogue, n_extra=len(rows) + len(tiles)),
        grid=(m // tm, n_out // tn),
        in_specs=in_specs,
        out_specs=pl.BlockSpec((tm, tn), lambda i, j: (i, j)),
        out_shape=jax.ShapeDtypeStruct((m, n_out), out_dtype),
        compiler_params=_params("parallel", "arbitrary"),
        name=name,
    )(x, w, *[r.reshape(1, n_out).astype(F32) for r in rows], *tiles)


def _seg_rms_epilogue(acc, g, *, seg, scale):
    ones = _seg_ones(V7X_LANES, seg)
    outs = []
    for s in range(acc.shape[1] // V7X_LANES):
        a = acc[:, s * V7X_LANES:(s + 1) * V7X_LANES]
        ss = _split_dot(a * a, ones)
        outs.append(a * lax.rsqrt(ss * (1.0 / seg) + RMS_EPS))
    return jnp.concatenate(outs, axis=1) * (g * scale)


def _wkv_kernel(r_ref, lw_ref, k_ref, v_ref, a_ref, g_ref, kk_ref, ka_ref, rk_ref, gg_ref, gb_ref, o_ref, h_ref,
                *, chunk, pairs):
    c2 = 2 * chunk
    half = V7X_LANES // 2
    assert c2 == V7X_LANES, "two stacked heads of one chunk must fill one 128-row tile"

    @pl.when(pl.program_id(2) == 0)
    def _():
        h_ref[...] = jnp.zeros_like(h_ref)

    lane = lax.broadcasted_iota(jnp.int32, (chunk, V7X_LANES), 1)
    head0 = lane < half
    ri = lax.broadcasted_iota(jnp.int32, (c2, c2), 0)
    ci = lax.broadcasted_iota(jnp.int32, (c2, c2), 1)
    same = (ri // chunk) == (ci // chunk)
    strict = jnp.logical_and(same, (ci % chunk) < (ri % chunk))
    incl = jnp.logical_and(same, (ci % chunk) <= (ri % chunk))
    eye = jnp.where(ri == ci, 1.0, 0.0)
    li = lax.broadcasted_iota(jnp.int32, (chunk, chunk), 0)
    lj = lax.broadcasted_iota(jnp.int32, (chunk, chunk), 1)
    ltri = jnp.where(li >= lj, 1.0, 0.0).astype(BF16)
    seg = _seg_ones(V7X_LANES, RWKV_HEAD)
    n_double = int(math.log2(chunk)) - 1

    def stack(x):
        return jnp.concatenate([jnp.where(head0, x, 0.0), jnp.where(head0, 0.0, x)], axis=0)

    def hi_lo(x, axis):
        hi = x.astype(BF16)
        return jnp.concatenate([hi, (x - hi.astype(F32)).astype(BF16)], axis=axis)

    def seg_sums(xs):
        prods = [jnp.dot(hi_lo(x, 0), seg, preferred_element_type=F32) for x in xs]
        return [o[:chunk] + o[chunk:] for o in prods]

    lanes = V7X_LANES
    P = range(pairs)
    cols = [slice(p * lanes, (p + 1) * lanes) for p in P]
    zeros = jnp.zeros((c2, lanes), F32)

    def body(c, carry):
        rows = pl.ds(pl.multiple_of(c * chunk, chunk), chunk)
        ld = lambda ref: [ref[rows, cs].astype(F32) for cs in cols]
        par = lambda ref: [ref[:, cs] for cs in cols]
        r, lw, k, v, a = ld(r_ref), ld(lw_ref), ld(k_ref), ld(v_ref), ld(a_ref)
        k_k, k_a, r_k = par(kk_ref), par(ka_ref), par(rk_ref)

        kk = [k[p] * k_k[p] for p in P]
        ss = seg_sums([x * x for x in kk])
        kk = [kk[p] * lax.rsqrt(jnp.maximum(ss[p], 1e-24)) for p in P]
        k = [k[p] * (1.0 + (a[p] - 1.0) * k_a[p]) for p in P]
        b = [kk[p] * a[p] for p in P]
        bonus = seg_sums([r[p] * k[p] * r_k[p] for p in P])

        cc = [jnp.dot(ltri, hi_lo(lw[p], 1), preferred_element_type=F32) for p in P]
        cum = [x[:, :lanes] + x[:, lanes:] for x in cc]
        mid = [x[chunk // 2 - 1:chunk // 2] for x in cum]
        last = [x[chunk - 1:chunk] for x in cum]
        dec_mid = [jnp.exp(mid[p] - cum[p]) for p in P]
        dec_last = [jnp.exp(last[p] - cum[p]) for p in P]
        emid = [jnp.exp(x) for x in mid]
        plast = [jnp.exp(x) for x in last]
        As = [stack(-kk[p] * jnp.exp(cum[p] - lw[p] - mid[p])) for p in P]
        Rs = [stack(r[p] * jnp.exp(cum[p] - mid[p])) for p in P]
        Vs = [stack(v[p]) for p in P]
        AR = [jnp.concatenate([As[p], Rs[p]], axis=0) for p in P]
        BK = [jnp.concatenate([stack(b[p] * dec_mid[p]), stack(k[p] * dec_mid[p])], axis=0) for p in P]
        BKh = [jnp.concatenate([stack(b[p] * dec_last[p]), stack(k[p] * dec_last[p])], axis=0) for p in P]

        G = [_mm_nt(AR[p], BK[p]) for p in P]
        Aab = [jnp.where(strict, g[:c2, :c2], 0.0) for g in G]
        Aak = [jnp.where(strict, g[:c2, c2:], 0.0) for g in G]
        S = [jnp.concatenate([jnp.where(incl, g[c2:, :c2], 0.0), jnp.where(incl, g[c2:, c2:], 0.0)], axis=1)
             for g in G]

        X = [eye + x for x in Aab]
        Q = [_mm(x, x) for x in Aab]
        for _ in range(n_double - 1):
            QX = [_mm(Q[p], jnp.concatenate([Q[p], X[p]], axis=1)) for p in P]
            X = [X[p] + QX[p][:, lanes:] for p in P]
            Q = [x[:, :lanes] for x in QX]
        X = [X[p] + _mm(Q[p], X[p]) for p in P]

        AV = [_mm(Aak[p], Vs[p]) for p in P]
        TA = [_mm(X[p], jnp.concatenate([As[p], AV[p]], axis=1)) for p in P]
        lowV = [jnp.concatenate([zeros, Vs[p]], axis=1) for p in P]
        SY = [_mm(S[p], jnp.concatenate([TA[p], lowV[p]], axis=0)) for p in P]
        Rbar = [(Rs[p] + SY[p][:, :lanes]) * emid[p] for p in P]
        AU = [jnp.concatenate([TA[p][:, :lanes] * emid[p], TA[p][:, lanes:]], axis=1) for p in P]
        BKT = [jnp.transpose(x) for x in BKh]
        MN = [_mm(BKT[p], jnp.concatenate([AU[p], lowV[p]], axis=0)) for p in P]
        M = [eye * plast[p] + MN[p][:, :lanes] for p in P]
        H = [h_ref[p] for p in P]
        RH = [_mm(jnp.concatenate([Rbar[p], M[p]], axis=0), H[p]) for p in P]
        for p in P:
            h_ref[p] = RH[p][c2:] + MN[p][:, lanes:]
        Ys = [RH[p][:c2] + SY[p][:, lanes:] for p in P]
        y = [x[:chunk] + x[chunk:] for x in Ys]

        inv = 1.0 / RWKV_HEAD
        mean = seg_sums(y)
        yc = [y[p] - mean[p] * inv for p in P]
        var = seg_sums([x * x for x in yc])
        gn_g, gn_b = par(gg_ref), par(gb_ref)
        for p in P:
            yn = yc[p] * lax.rsqrt(var[p] * inv + GN_EPS) * gn_g[p] + gn_b[p]
            gate = g_ref[rows, cols[p]].astype(F32)
            o_ref[rows, cols[p]] = ((yn + bonus[p] * v[p]) * gate).astype(o_ref.dtype)
        return carry

    lax.fori_loop(0, r_ref.shape[0] // chunk, body, 0)


def wkv7(r, lw, k, v, a, g, k_k, k_a, r_k, gn_g, gn_b, batch, seq):
    n, d = r.shape
    pairs = min(WKV_PAIRS, d // V7X_LANES)
    width = pairs * V7X_LANES
    tb = _tile(seq, 256)
    seq_spec = pl.BlockSpec((tb, width), lambda b, p, t: (b * (seq // tb) + t, p))
    par_spec = pl.BlockSpec((1, width), lambda b, p, t: (0, p))
    row = lambda x: x.reshape(1, d).astype(F32)
    return pl.pallas_call(
        functools.partial(_wkv_kernel, chunk=WKV_CHUNK, pairs=pairs),
        grid=(batch, d // width, seq // tb),
        in_specs=[seq_spec] * 6 + [par_spec] * 5,
        out_specs=seq_spec,
        out_shape=jax.ShapeDtypeStruct((n, d), BF16),
        scratch_shapes=[pltpu.VMEM((pairs, V7X_LANES, V7X_LANES), F32)],
        compiler_params=_params("parallel", "parallel", "arbitrary"),
        name="wkv7",
    )(r, lw, k, v, a, g, row(k_k), row(k_a), row(r_k), row(gn_g), row(gn_b))


def _decay_epilogue(acc, w0):
    z = -(w0 + acc)
    softplus = jnp.maximum(z, 0.0) + jnp.log(1.0 + jnp.exp(-jnp.abs(z)))
    return -jnp.exp(-softplus - 0.5)


def _rwkv_prep_kernel(x_ref, prev_ref, g_ref, mix_ref, w1_ref, a1_ref, g1_ref, xs_ref, tw_ref, ta_ref, tg_ref,
                      *, tm, seq):
    i = pl.program_id(0)
    g = g_ref[...]
    xn = _rms(x_ref[...], g)
    prev = _rms(prev_ref[...], g)[V7X_SUBLANES - 1:V7X_SUBLANES]
    prev = jnp.where((i * tm) % seq == 0, 0.0, prev)
    row = lax.broadcasted_iota(jnp.int32, xn.shape, 0)
    xx = jnp.where(row == 0, prev, pltpu.roll(xn, 1, 0)) - xn
    mixed = lambda s: (xn + xx * mix_ref[s:s + 1, :]).astype(BF16)
    for slot, s in enumerate((0, 2, 3)):
        xs_ref[slot] = mixed(s)
    dot = lambda a, w_ref: jnp.dot(a, w_ref[...], preferred_element_type=F32)
    tw_ref[...] = jnp.tanh(dot(mixed(1), w1_ref)).astype(tw_ref.dtype)
    ta_ref[...] = dot(mixed(4), a1_ref).astype(ta_ref.dtype)
    tg_ref[...] = jax.nn.sigmoid(dot(mixed(5), g1_ref)).astype(tg_ref.dtype)


def rwkv_prep(h, norm_g, mix, w1, a1, g1, seq):
    n, d = h.shape
    tm = _tile(seq, 256)
    per = tm // V7X_SUBLANES
    narrow = [jnp.pad(w, ((0, 0), (0, -w.shape[1] % V7X_LANES))).astype(BF16) for w in (w1, a1, g1)]
    whole = lambda arr: pl.BlockSpec(arr.shape, lambda i: (0, 0))
    return pl.pallas_call(
        functools.partial(_rwkv_prep_kernel, tm=tm, seq=seq),
        grid=(n // tm,),
        in_specs=[pl.BlockSpec((tm, d), lambda i: (i, 0)),
                  pl.BlockSpec((V7X_SUBLANES, d), lambda i: (jnp.maximum(i * per - 1, 0), 0)),
                  pl.BlockSpec((1, d), lambda i: (0, 0)), whole(mix)] + [whole(w) for w in narrow],
        out_specs=[pl.BlockSpec((3, tm, d), lambda i: (0, i, 0))]
        + [pl.BlockSpec((tm, w.shape[1]), lambda i: (i, 0)) for w in narrow],
        out_shape=[jax.ShapeDtypeStruct((3, n, d), BF16)]
        + [jax.ShapeDtypeStruct((n, w.shape[1]), BF16) for w in narrow],
        compiler_params=_params("parallel"),
        name="rwkv_prep",
    )(h, h, norm_g.reshape(1, d), mix, *narrow)


def rwkv_layer(h, batch, seq, norm_g, mix, w_r, w_k, w_v, w0, w1, w2, a0, a1, a2, g1, g2, k_k, k_a, r_k, gn_g,
               gn_b, w_o):
    bf = lambda w: w.astype(BF16)
    pad_rows = lambda w: jnp.pad(w, ((0, -w.shape[0] % V7X_LANES), (0, 0)))
    xs, tw, ta, tg = rwkv_prep(h, norm_g, mix, w1, a1, g1, seq)
    r = matmul(xs, bf(w_r), x_sel=0, out_dtype=BF16, name="rwkv_r")
    k = matmul(xs, bf(w_k), x_sel=1, out_dtype=BF16, name="rwkv_k")
    v = matmul(xs, bf(w_v), x_sel=2, out_dtype=BF16, name="rwkv_v")
    lw = matmul(tw, bf(pad_rows(w2)), out_dtype=F32, rows=(w0,), name="rwkv_w2", epilogue=_decay_epilogue)
    a = matmul(ta, bf(pad_rows(a2)), out_dtype=BF16, rows=(a0,), name="rwkv_a2",
               epilogue=lambda acc, b: jax.nn.sigmoid(b + acc))
    g = matmul(tg, bf(pad_rows(g2)), out_dtype=BF16, name="rwkv_g2")
    y = wkv7(r, lw, k, v, a, g, k_k, k_a, r_k, gn_g, gn_b, batch, seq)
    return matmul(y, bf(w_o), out_dtype=F32, tiles=(h,), epilogue=lambda acc, res: res + acc, name="rwkv_o")


LOG2_E = 1.4426950408889634
ATTN_UNSHIFTED_MAX_LOG2 = 80.0
ATTN_UNROLL = 4
ATTN_LAG = 1
ATTN_SUM_ROWS = 16


def _sub_head_queries(q):
    lane = lax.broadcasted_iota(jnp.int32, q.shape, 1)
    zero = jnp.zeros_like(q)
    return jnp.where(lane < DIFF_HEAD, q, zero), jnp.where(lane < DIFF_HEAD, zero, q)


def _causal_tile(tq):
    return lax.broadcasted_iota(jnp.int32, (tq, tq), 0) >= lax.broadcasted_iota(jnp.int32, (tq, tq), 1)


def _attn_online_kernel(lam_ref, q_ref, k_ref, v_ref, sg_ref, o_ref, *, tq, out_scale):
    qi = pl.program_id(2)
    qs = _sub_head_queries(q_ref[...])
    causal = _causal_tile(tq)

    def step(j, carry, masked):
        start = pl.multiple_of(j * tq, tq)
        kj = k_ref[pl.ds(start, tq), :]
        vj = v_ref[pl.ds(start, tq), :]
        out = []
        for sub in range(2):
            m, l, acc = carry[sub]
            s = lax.dot_general(qs[sub], kj, (((1,), (1,)), ((), ())), preferred_element_type=F32)
            if masked:
                s = jnp.where(causal, s, -1e30)
            m_new = jnp.maximum(m, jnp.max(s, axis=-1, keepdims=True))
            alpha = jnp.exp2(m - m_new)
            p = jnp.exp2(s - m_new)
            l = alpha * l + jnp.sum(p, axis=-1, keepdims=True)
            acc = alpha * acc + jnp.dot(p.astype(BF16), vj, preferred_element_type=F32)
            out.append((m_new, l, acc))
        return tuple(out)

    init = tuple((jnp.full((tq, 1), -1e30, F32), jnp.zeros((tq, 1), F32), jnp.zeros((tq, V7X_LANES), F32))
                 for _ in range(2))
    carry = lax.fori_loop(0, qi, lambda j, c: step(j, c, False), init)
    (_, l0, acc0), (_, l1, acc1) = step(qi, carry, True)
    o = acc0 / l0 - lam_ref[0] * (acc1 / l1)
    o_ref[...] = (_rms(o, sg_ref[...]) * out_scale).astype(o_ref.dtype)


def _attn_unshifted_kernel(lam_ref, q_ref, k_ref, vt_ref, sg_ref, o_ref, acc_ref, *, tq, out_scale):
    qi = pl.program_id(2)
    qs = _sub_head_queries(q_ref[...])
    visible = lax.broadcasted_iota(jnp.int32, (tq, tq), 1) >= lax.broadcasted_iota(jnp.int32, (tq, tq), 0)
    ones = jnp.ones((ATTN_SUM_ROWS, tq), BF16)
    acc_ref[...] = jnp.zeros_like(acc_ref)

    def scores(j, sub, masked):
        kj = k_ref[pl.ds(pl.multiple_of(j * tq, tq), tq), :]
        st = lax.dot_general(kj, qs[sub], (((1,), (1,)), ((), ())), preferred_element_type=F32)
        return jnp.where(visible, st, -1e30) if masked else st

    def accumulate(j, sub, st):
        vt = jnp.concatenate([vt_ref[j], ones], axis=0)
        acc_ref[sub] += jnp.dot(vt, jnp.exp2(st).astype(BF16), preferred_element_type=F32)

    def steps(tiles):
        work = [(j, sub, masked) for j, masked in tiles for sub in range(2)]
        pending = []
        for j, sub, masked in work:
            pending.append((j, sub, scores(j, sub, masked)))
            if len(pending) > ATTN_LAG:
                accumulate(*pending.pop(0))
        for item in pending:
            accumulate(*item)

    def body(jj, c):
        steps([(jj * ATTN_UNROLL + u, False) for u in range(ATTN_UNROLL)])
        return c

    n_full = qi // ATTN_UNROLL
    lax.fori_loop(0, n_full, body, 0)
    first = n_full * ATTN_UNROLL
    for rem in range(ATTN_UNROLL):
        @pl.when(qi - first == rem)
        def _():
            steps([(first + u, False) for u in range(rem)] + [(qi, True)])
    a0 = acc_ref[0]
    a1 = acc_ref[1]
    hw = 2 * DIFF_HEAD
    ot = a0[:hw] / a0[hw:hw + 1] - lam_ref[0] * (a1[:hw] / a1[hw:hw + 1])
    ot = ot * lax.rsqrt(jnp.mean(ot * ot, axis=0, keepdims=True) + RMS_EPS)
    o_ref[...] = (jnp.transpose(ot) * (sg_ref[...] * out_scale)).astype(o_ref.dtype)


def diff_attention_core(q, k, v, lam, subln_g, batch, seq, lam_init, unshifted):
    n, d = q.shape
    tq = _tile(seq, 512)
    nq = seq // tq
    head_w = 2 * DIFF_HEAD
    q_spec = pl.BlockSpec((tq, head_w), lambda b, h, i: (b * nq + i, h))
    kv_spec = pl.BlockSpec((seq, head_w), lambda b, h, i: (b, h))
    if unshifted:
        body = _attn_unshifted_kernel
        scratch = [pltpu.VMEM((2, head_w + ATTN_SUM_ROWS, tq), F32)]
        v = jnp.transpose(v.reshape(n // tq, tq, d), (0, 2, 1))
        v_spec = pl.BlockSpec((nq, head_w, tq), lambda b, h, i: (b, h, 0))
    else:
        body = _attn_online_kernel
        scratch = []
        v_spec = kv_spec
    return pl.pallas_call(
        functools.partial(body, tq=tq, out_scale=1.0 - lam_init),
        grid=(batch, d // head_w, nq),
        in_specs=[pl.BlockSpec(memory_space=pltpu.SMEM), q_spec, kv_spec, v_spec,
                  pl.BlockSpec((1, head_w), lambda b, h, i: (0, 0))],
        out_specs=q_spec,
        out_shape=jax.ShapeDtypeStruct((n, d), BF16),
        scratch_shapes=scratch,
        compiler_params=_params("parallel", "parallel", "arbitrary"),
        name="diff_attn_unshifted" if unshifted else "diff_attn_online",
    )(lam.reshape(1).astype(F32), q, k, v, subln_g.reshape(1, head_w).astype(F32))


def shared_kv(h, kv_norm_g, w_kv, k_norm_g):
    d = h.shape[1]
    hn = rmsnorm(h, kv_norm_g)
    kg = jnp.tile(k_norm_g, d // DIFF_HEAD)
    w = w_kv.astype(BF16)
    k = matmul(hn, w, w_cols=(0, d), out_dtype=BF16, rows=(kg,), name="kv_k", tn=512,
               epilogue=functools.partial(_seg_rms_epilogue, seg=DIFF_HEAD, scale=1.0))
    v = matmul(hn, w, w_cols=(d, d), out_dtype=BF16, name="kv_v")
    return k, v


def attn_layer(h, k, v, batch, seq, layer_idx, norm_g, w_q, q_norm_g, k_norm_g, lq1, lk1, lq2, lk2, subln_g, w_o):
    d = h.shape[1]
    hn = rmsnorm(h, norm_g)
    qg = jnp.tile(q_norm_g, d // DIFF_HEAD)
    q_scale = LOG2_E * DIFF_HEAD ** -0.5
    q = matmul(hn, w_q.astype(BF16), out_dtype=BF16, rows=(qg,), name="attn_q", tn=512,
               epilogue=functools.partial(_seg_rms_epilogue, seg=DIFF_HEAD, scale=q_scale))
    lam_init = 0.8 - 0.6 * math.exp(-0.3 * layer_idx)
    lam = jnp.exp(jnp.sum(lq1 * lk1)) - jnp.exp(jnp.sum(lq2 * lk2)) + lam_init
    bound = 1.01 * q_scale * DIFF_HEAD * jnp.max(jnp.abs(q_norm_g)) * jnp.max(jnp.abs(k_norm_g))
    core = lambda unshifted: functools.partial(diff_attention_core, batch=batch, seq=seq, lam_init=lam_init,
                                               unshifted=unshifted)
    o = lax.cond(bound <= ATTN_UNSHIFTED_MAX_LOG2, core(True), core(False), q, k, v, lam, subln_g)
    return matmul(o, w_o.astype(BF16), out_dtype=F32, tiles=(h,), epilogue=lambda acc, res: res + acc,
                  name="attn_o")


PACK_ROWS = V7X_SUBLANES
ROUTE_ROWS = V7X_SUBLANES
DMA_LOOP_UNROLL = 8
DMA_PRIORITIES = 2
EXPERT_WEIGHT_CHUNK_BYTES = 1 << 20
EXPERT_STAGE_SLOTS = 4
EXPERT_PHASES = 4


def _pack_store(x, o_ref):
    m = x.shape[0]
    assert x.shape[1] == 2 * PACK_ROWS * V7X_LANES
    word = lambda j: pltpu.bitcast(x[:, j * V7X_LANES:(j + 1) * V7X_LANES].astype(BF16).astype(F32), jnp.uint32)
    for s in range(PACK_ROWS):
        o_ref[pl.ds(s, m, stride=PACK_ROWS), :] = word(s) | (word(s + PACK_ROWS) >> 16)


def _unpack_load(p_ref, m):
    words = [p_ref[pl.ds(s, m, stride=PACK_ROWS), :] for s in range(PACK_ROWS)]
    hi = [pltpu.bitcast(w & jnp.uint32(0xFFFF0000), F32) for w in words]
    lo = [pltpu.bitcast(w << 16, F32) for w in words]
    return jnp.concatenate(hi + lo, axis=1)


def _router_kernel(x_ref, g_ref, w_ref, b_ref, t_ref, route_ref, *, n_groups, per_group):
    t = _rms(x_ref[...], g_ref[...])
    _pack_store(t, t_ref)
    logits = jnp.dot(t, w_ref[...], preferred_element_type=F32, precision=lax.Precision.HIGHEST) + b_ref[...]
    lane = lax.broadcasted_iota(jnp.int32, logits.shape, 1)
    none = jnp.int32(logits.shape[1])
    neg = -jnp.inf
    rmax = lambda x: jnp.max(x, axis=-1, keepdims=True)
    first = lambda hit: jnp.min(jnp.where(hit, lane, none), axis=-1, keepdims=True)

    is_group = lane < n_groups
    g_logit = jnp.where(is_group, logits, neg)
    g_max = rmax(g_logit)
    grp = first(g_logit == g_max)
    p_group = 1.0 / jnp.sum(jnp.where(is_group, jnp.exp(logits - g_max), 0.0), axis=-1, keepdims=True)

    lo = n_groups + grp * per_group
    e_logit = jnp.where(jnp.logical_and(lane >= lo, lane < lo + per_group), logits, neg)
    top1 = rmax(e_logit)
    i1 = first(e_logit == top1)
    e_rest = jnp.where(lane == i1, neg, e_logit)
    top2 = rmax(e_rest)
    i2 = first(e_rest == top2)
    ratio = jnp.exp(top2 - top1)
    gate1 = p_group / (1.0 + ratio)
    gate2 = gate1 * ratio

    cols = [(i1 - n_groups).astype(F32), (i2 - n_groups).astype(F32), gate1, gate2]
    out = jnp.zeros_like(logits)
    for c, val in enumerate(cols):
        out = jnp.where(lane == c, val, out)
    route_ref[...] = jnp.transpose(out)[:ROUTE_ROWS]


def moe_router(h, norm_g, rg_w, rg_b, re_w, re_b):
    n, d = h.shape
    tm = _tile(n, 256)
    n_groups, n_exp = rg_w.shape[1], re_w.shape[1]
    assert n_groups + n_exp <= V7X_LANES
    pad = V7X_LANES - n_groups - n_exp
    w = jnp.pad(jnp.concatenate([rg_w, re_w], axis=1), ((0, 0), (0, pad)))
    b = jnp.pad(jnp.concatenate([rg_b, re_b]), (0, pad)).reshape(1, -1)
    return pl.pallas_call(
        functools.partial(_router_kernel, n_groups=n_groups, per_group=n_exp // n_groups),
        grid=(n // tm,),
        in_specs=[pl.BlockSpec((tm, d), lambda i: (i, 0)), pl.BlockSpec((1, d), lambda i: (0, 0)),
                  pl.BlockSpec((d, V7X_LANES), lambda i: (0, 0)), pl.BlockSpec((1, V7X_LANES), lambda i: (0, 0))],
        out_specs=[pl.BlockSpec((tm * PACK_ROWS, V7X_LANES), lambda i: (i, 0)),
                   pl.BlockSpec((ROUTE_ROWS, tm), lambda i: (0, i))],
        out_shape=[jax.ShapeDtypeStruct((n * PACK_ROWS, V7X_LANES), jnp.uint32),
                   jax.ShapeDtypeStruct((ROUTE_ROWS, n), F32)],
        compiler_params=_params("parallel"),
        name="moe_router",
    )(h, norm_g.reshape(1, d), w, b)


def _token_copy(src_hbm, token, dst, slot, sem):
    src = src_hbm.at[pl.ds(pl.multiple_of(token * PACK_ROWS, PACK_ROWS), PACK_ROWS), :]
    return pltpu.make_async_copy(src, dst.at[pl.ds(pl.multiple_of(slot * PACK_ROWS, PACK_ROWS), PACK_ROWS), :], sem)


def _expert_kernel(be_ref, nused_ref, idx_ref, idx_next_ref, x_hbm, wg_hbm, wu_hbm, wd_hbm, rw_ref, o_ref,
                   xbuf, wg_v, wu_v, wd_v, stage_in, stage_out, xsem, wsem, *, bm, layer, n_blocks):
    i = pl.program_id(0)
    slot = i % 2
    e = be_ref[i]
    used = i < nused_ref[0]
    new_expert = jnp.logical_or(i == 0, e != be_ref[jnp.maximum(i - 1, 0)])

    def gather(idx, s, first=0, last=bm):
        for r in range(first, last):
            _token_copy(x_hbm, idx[0, 0, r], xbuf.at[s], r, xsem.at[s]).start(priority=r % DMA_PRIORITIES)

    def drain(s):
        def body(r, c):
            _token_copy(x_hbm, 0, xbuf.at[s], r, xsem.at[s]).wait()
            return c
        lax.fori_loop(0, bm, body, 0, unroll=DMA_LOOP_UNROLL)

    def load(w_hbm, w_v, stage):
        slots, chunk = stage.shape[0], stage.shape[1]
        n_chunks = w_v.shape[0] // chunk
        copy = lambda c: pltpu.make_async_copy(w_hbm.at[layer, e, pl.ds(c * chunk, chunk), :], stage.at[c % slots],
                                               wsem.at[c % slots])
        for c in range(min(slots - 1, n_chunks)):
            copy(c).start()
        for c in range(n_chunks):
            if c + slots - 1 < n_chunks:
                copy(c + slots - 1).start()
            copy(c).wait()
            w_v[pl.ds(c * chunk, chunk), :] = stage[c % slots].astype(BF16)

    @pl.when(i == 0)
    def _():
        gather(idx_ref, 0)

    drain(slot)

    @pl.when(jnp.logical_and(used, new_expert))
    def _():
        load(wg_hbm, wg_v, stage_in)
        load(wu_hbm, wu_v, stage_in)
        load(wd_hbm, wd_v, stage_out)

    @pl.when(used)
    def _():
        x = _unpack_load(xbuf.at[slot], bm).astype(BF16)
        hc = wg_v.shape[1] // EXPERT_PHASES
        rows = bm // EXPERT_PHASES
        acc = None
        for c in range(EXPERT_PHASES):
            gather(idx_next_ref, 1 - slot, c * rows, (c + 1) * rows)
            hg = jnp.dot(x, wg_v[:, c * hc:(c + 1) * hc], preferred_element_type=F32)
            hu = jnp.dot(x, wu_v[:, c * hc:(c + 1) * hc], preferred_element_type=F32)
            act = (hg * jax.nn.sigmoid(hg) * hu).astype(BF16)
            part = jnp.dot(act, wd_v[c * hc:(c + 1) * hc, :], preferred_element_type=F32)
            acc = part if acc is None else acc + part
        _pack_store(acc * rw_ref[...], o_ref)

    @pl.when(jnp.logical_not(used))
    def _():
        gather(idx_next_ref, 1 - slot)
        o_ref[...] = jnp.zeros_like(o_ref)

    @pl.when(i == n_blocks - 1)
    def _():
        drain(1 - slot)


def moe_experts(x_packed, buf_tok, block_e, n_used, row_w, wg, wu, wd, layer, bm):
    n_pad = row_w.shape[0]
    n_blocks = n_pad // bm
    d, hid = wg.shape[2], wg.shape[3]
    chunk_in = EXPERT_WEIGHT_CHUNK_BYTES // (hid * 4)
    chunk_out = EXPERT_WEIGHT_CHUNK_BYTES // (d * 4)
    assert d % chunk_in == 0 and hid % chunk_out == 0
    packed = pl.BlockSpec((bm * PACK_ROWS, V7X_LANES), lambda i, be, nu: (i, 0))
    hbm = pl.BlockSpec(memory_space=pl.ANY)
    idx = buf_tok.reshape(n_blocks, 1, bm)
    return pl.pallas_call(
        functools.partial(_expert_kernel, bm=bm, layer=layer, n_blocks=n_blocks),
        grid_spec=pltpu.PrefetchScalarGridSpec(
            num_scalar_prefetch=2,
            grid=(n_blocks,),
            in_specs=[
                pl.BlockSpec((1, 1, bm), lambda i, be, nu: (i, 0, 0), memory_space=pltpu.SMEM),
                pl.BlockSpec((1, 1, bm), lambda i, be, nu: (jnp.minimum(i + 1, n_blocks - 1), 0, 0),
                             memory_space=pltpu.SMEM),
                hbm, hbm, hbm, hbm, pl.BlockSpec((bm, 1), lambda i, be, nu: (i, 0))],
            out_specs=packed,
            scratch_shapes=[
                pltpu.VMEM((2, bm * PACK_ROWS, V7X_LANES), jnp.uint32),
                pltpu.VMEM((d, hid), BF16), pltpu.VMEM((d, hid), BF16), pltpu.VMEM((hid, d), BF16),
                pltpu.VMEM((EXPERT_STAGE_SLOTS, chunk_in, hid), F32),
                pltpu.VMEM((EXPERT_STAGE_SLOTS, chunk_out, d), F32),
                pltpu.SemaphoreType.DMA((2,)), pltpu.SemaphoreType.DMA((EXPERT_STAGE_SLOTS,)),
            ],
        ),
        out_shape=jax.ShapeDtypeStruct((n_pad * PACK_ROWS, V7X_LANES), jnp.uint32),
        compiler_params=_params("arbitrary"),
        name="moe_experts",
    )(block_e, n_used, idx, idx, x_packed, wg, wu, wd, row_w.reshape(n_pad, 1))


def _combine_kernel(idx_ref, yb_hbm, h_ref, o_ref, buf, sem, *, rows):
    def issue(r, c):
        for s in range(TOP_K):
            _token_copy(yb_hbm, idx_ref[0, s, r], buf.at[s], r, sem).start(priority=s % DMA_PRIORITIES)
        return c

    def drain(r, c):
        for s in range(TOP_K):
            _token_copy(yb_hbm, 0, buf.at[s], r, sem).wait()
        return c

    lax.fori_loop(0, rows, issue, 0, unroll=DMA_LOOP_UNROLL)
    lax.fori_loop(0, rows, drain, 0, unroll=DMA_LOOP_UNROLL)
    acc = h_ref[...]
    for s in range(TOP_K):
        acc = acc + _unpack_load(buf.at[s], rows)
    o_ref[...] = acc


def moe_combine(h, yb, pos, rows_per_step=256):
    n, d = h.shape
    tc = _tile(n, rows_per_step)
    idx = pos.reshape(n // tc, tc, TOP_K).transpose(0, 2, 1)
    return pl.pallas_call(
        functools.partial(_combine_kernel, rows=tc),
        grid=(n // tc,),
        in_specs=[pl.BlockSpec((1, TOP_K, tc), lambda i: (i, 0, 0), memory_space=pltpu.SMEM),
                  pl.BlockSpec(memory_space=pl.ANY),
                  pl.BlockSpec((tc, d), lambda i: (i, 0))],
        out_specs=pl.BlockSpec((tc, d), lambda i: (i, 0)),
        out_shape=jax.ShapeDtypeStruct((n, d), F32),
        scratch_shapes=[pltpu.VMEM((TOP_K, tc * PACK_ROWS, V7X_LANES), jnp.uint32), pltpu.SemaphoreType.DMA(())],
        compiler_params=_params("arbitrary"),
        name="moe_combine",
    )(idx, yb, h)


def moe_layer(h, norm_g, rg_w, rg_b, re_w, re_b, wg, wu, wd, layer, bm=256):
    n, d = h.shape
    n_exp = wg.shape[1]
    t, route = moe_router(h, norm_g, rg_w, rg_b, re_w, re_b)
    flat_e = jnp.transpose(route[:TOP_K]).astype(jnp.int32).reshape(-1)
    gate = jnp.transpose(route[TOP_K:2 * TOP_K])

    n_assign = n * TOP_K
    n_pad = n_assign + n_exp * bm
    order = jnp.argsort(flat_e).astype(jnp.int32)
    rank = jnp.argsort(order).astype(jnp.int32)
    experts = jnp.arange(n_exp, dtype=jnp.int32)
    counts = jnp.sum(flat_e[None, :] == experts[:, None], axis=1, dtype=jnp.int32)
    padded = (counts + bm - 1) // bm * bm
    padded_end = jnp.cumsum(padded)
    padded_start = padded_end - padded
    start = jnp.cumsum(counts) - counts
    shift = padded_start - start
    pos = (rank + jnp.sum(jnp.where(flat_e[None, :] == experts[:, None], shift[:, None], 0), axis=0)).reshape(n, TOP_K)
    n_blocks = n_pad // bm
    block_start = jnp.arange(n_blocks, dtype=jnp.int32) * bm
    block_e = jnp.sum(block_start[:, None] >= padded_end[None, :], axis=1, dtype=jnp.int32)
    block_e = jnp.minimum(block_e, n_exp - 1)
    n_used = (padded_end[-1] // bm).astype(jnp.int32).reshape(1)
    per_row = lambda table: jnp.repeat(table[block_e], bm)
    off = jnp.arange(n_pad, dtype=jnp.int32) - per_row(padded_start)
    valid = off < per_row(counts)
    src = order[jnp.clip(per_row(start) + off, 0, n_assign - 1)]
    buf_tok = jnp.where(valid, src // TOP_K, 0)
    buf_w = jnp.where(valid, gate.reshape(-1)[src], 0.0)

    yb = moe_experts(t, buf_tok, block_e, n_used, buf_w, wg, wu, wd, layer, bm)
    return moe_combine(h, yb, pos)


def kernel(x, rwkv_norm_g, rwkv_mix, rwkv_w_r, rwkv_w_k, rwkv_w_v, rwkv_w0, rwkv_w1, rwkv_w2, rwkv_a0, rwkv_a1,
           rwkv_a2, rwkv_g1, rwkv_g2, rwkv_k_k, rwkv_k_a, rwkv_r_k, rwkv_gn_g, rwkv_gn_b, rwkv_w_o, kv_norm_g,
           w_kv, k_norm_g, attn_norm_g, attn_w_q, q_norm_g, lambda_q1, lambda_k1, lambda_q2, lambda_k2, subln_g,
           attn_w_o, moe_norm_g, router_group_w, router_group_b, router_expert_w, router_expert_b, expert_w_gate,
           expert_w_up, expert_w_down):
    batch, seq, d = x.shape
    depth = moe_norm_g.shape[0]
    n_a = rwkv_norm_g.shape[0]
    h = x.reshape(batch * seq, d)
    k_shared = v_shared = None
    for l in range(depth):
        if l < n_a:
            i = l
            h = rwkv_layer(h, batch, seq, rwkv_norm_g[i], rwkv_mix[i], rwkv_w_r[i], rwkv_w_k[i], rwkv_w_v[i],
                           rwkv_w0[i], rwkv_w1[i], rwkv_w2[i], rwkv_a0[i], rwkv_a1[i], rwkv_a2[i], rwkv_g1[i],
                           rwkv_g2[i], rwkv_k_k[i], rwkv_k_a[i], rwkv_r_k[i], rwkv_gn_g[i], rwkv_gn_b[i],
                           rwkv_w_o[i])
        else:
            j = l - n_a
            if j == 0:
                k_shared, v_shared = shared_kv(h, kv_norm_g, w_kv, k_norm_g)
            h = attn_layer(h, k_shared, v_shared, batch, seq, l, attn_norm_g[j], attn_w_q[j], q_norm_g[j], k_norm_g,
                           lambda_q1[j], lambda_k1[j], lambda_q2[j], lambda_k2[j], subln_g[j], attn_w_o[j])
        h = moe_layer(h, moe_norm_g[l], router_group_w[l], router_group_b[l], router_expert_w[l],
                      router_expert_b[l], expert_w_gate, expert_w_up, expert_w_down, l)
    return h.reshape(batch, seq, d)
```

```python
import functools
import math

import jax
import jax.numpy as jnp
from jax import lax
from jax.experimental import pallas as pl
from jax.experimental.pallas import tpu as pltpu

V7X_LANES = 128
V7X_SUBLANES = 8
V7X_VMEM_BYTES = 64 * 1024 * 1024
VMEM_LIMIT_BYTES = V7X_VMEM_BYTES - 8 * 1024 * 1024

RWKV_HEAD = 64
DIFF_HEAD = 64
TOP_K = 2
GN_EPS = 64e-5
RMS_EPS = 1e-6
WKV_CHUNK = 64
WKV_PAIRS = 16

BF16 = jnp.bfloat16
F32 = jnp.float32


def _tile(n, pref):
    t = min(n, pref)
    assert n % t == 0, (n, pref)
    return t


def _params(*sem):
    return pltpu.CompilerParams(dimension_semantics=sem, vmem_limit_bytes=VMEM_LIMIT_BYTES)


def _mm(a, b):
    return jnp.dot(a.astype(BF16), b.astype(BF16), preferred_element_type=F32)


def _mm_nt(a, b):
    return lax.dot_general(a.astype(BF16), b.astype(BF16), (((1,), (1,)), ((), ())), preferred_element_type=F32)


def _split_dot(x, w_bf16):
    hi = x.astype(BF16)
    lo = (x - hi.astype(F32)).astype(BF16)
    return (jnp.dot(hi, w_bf16, preferred_element_type=F32) + jnp.dot(lo, w_bf16, preferred_element_type=F32))


def _seg_ones(n, seg):
    i = lax.broadcasted_iota(jnp.int32, (n, n), 0) // seg
    j = lax.broadcasted_iota(jnp.int32, (n, n), 1) // seg
    return jnp.where(i == j, 1.0, 0.0).astype(BF16)


def _rms(x, g):
    return x * lax.rsqrt(jnp.mean(x * x, axis=-1, keepdims=True) + RMS_EPS) * g


def _rmsnorm_kernel(x_ref, g_ref, o_ref):
    o_ref[...] = _rms(x_ref[...], g_ref[...]).astype(o_ref.dtype)


def rmsnorm(x, g, out_dtype=BF16):
    n, d = x.shape
    tm = _tile(n, 512)
    return pl.pallas_call(
        _rmsnorm_kernel,
        grid=(n // tm,),
        in_specs=[pl.BlockSpec((tm, d), lambda i: (i, 0)), pl.BlockSpec((1, d), lambda i: (0, 0))],
        out_specs=pl.BlockSpec((tm, d), lambda i: (i, 0)),
        out_shape=jax.ShapeDtypeStruct((n, d), out_dtype),
        compiler_params=_params("parallel"),
        name="rmsnorm",
    )(x, g.reshape(1, d))


def _matmul_kernel(x_ref, w_ref, *rest, epilogue, n_extra):
    extra = [r[...] for r in rest[:n_extra]]
    o_ref = rest[n_extra]
    acc = jnp.dot(x_ref[...], w_ref[...], preferred_element_type=F32)
    o_ref[...] = epilogue(acc, *extra).astype(o_ref.dtype)


def matmul(x, w, *, out_dtype, epilogue=None, rows=(), tiles=(), x_sel=None, w_cols=None, tm=1024, tn=1024,
           name="matmul"):
    if x_sel is None:
        m, k = x.shape
        x_spec = lambda tm_: pl.BlockSpec((tm_, k), lambda i, j: (i, 0))
    else:
        _, m, k = x.shape
        x_spec = lambda tm_: pl.BlockSpec((None, tm_, k), lambda i, j: (x_sel, i, 0))
    col0, n_out = (0, w.shape[1]) if w_cols is None else w_cols
    tm = _tile(m, tm)
    tn = _tile(n_out, tn)
    assert col0 % tn == 0
    j0 = col0 // tn
    if epilogue is None:
        epilogue = lambda acc: acc
    in_specs = [x_spec(tm), pl.BlockSpec((k, tn), lambda i, j: (0, j + j0))]
    in_specs += [pl.BlockSpec((1, tn), lambda i, j: (0, j)) for _ in rows]
    in_specs += [pl.BlockSpec((tm, tn), lambda i, j: (i, j)) for _ in tiles]
    return pl.pallas_call(
        functools.partial(_matmul_kernel, epilogue=epilogue, n_extra=len(rows) + len(tiles)),
        grid=(m // tm, n_out // tn),
        in_specs=in_specs,
        out_specs=pl.BlockSpec((tm, tn), lambda i, j: (i, j)),
        out_shape=jax.ShapeDtypeStruct((m, n_out), out_dtype),
        compiler_params=_params("parallel", "arbitrary"),
        name=name,
    )(x, w, *[r.reshape(1, n_out).astype(F32) for r in rows], *tiles)


def _seg_rms_epilogue(acc, g, *, seg, scale):
    ones = _seg_ones(V7X_LANES, seg)
    outs = []
    for s in range(acc.shape[1] // V7X_LANES):
        a = acc[:, s * V7X_LANES:(s + 1) * V7X_LANES]
        ss = _split_dot(a * a, ones)
        outs.append(a * lax.rsqrt(ss * (1.0 / seg) + RMS_EPS))
    return jnp.concatenate(outs, axis=1) * (g * scale)


def _wkv_kernel(r_ref, lw_ref, k_ref, v_ref, a_ref, g_ref, kk_ref, ka_ref, rk_ref, gg_ref, gb_ref, o_ref, h_ref,
                *, chunk, pairs):
    c2 = 2 * chunk
    half = V7X_LANES // 2
    assert c2 == V7X_LANES, "two stacked heads of one chunk must fill one 128-row tile"

    @pl.when(pl.program_id(2) == 0)
    def _():
        h_ref[...] = jnp.zeros_like(h_ref)

    lane = lax.broadcasted_iota(jnp.int32, (chunk, V7X_LANES), 1)
    head0 = lane < half
    ri = lax.broadcasted_iota(jnp.int32, (c2, c2), 0)
    ci = lax.broadcasted_iota(jnp.int32, (c2, c2), 1)
    same = (ri // chunk) == (ci // chunk)
    strict = jnp.logical_and(same, (ci % chunk) < (ri % chunk))
    incl = jnp.logical_and(same, (ci % chunk) <= (ri % chunk))
    eye = jnp.where(ri == ci, 1.0, 0.0)
    li = lax.broadcasted_iota(jnp.int32, (chunk, chunk), 0)
    lj = lax.broadcasted_iota(jnp.int32, (chunk, chunk), 1)
    ltri = jnp.where(li >= lj, 1.0, 0.0).astype(BF16)
    seg = _seg_ones(V7X_LANES, RWKV_HEAD)
    n_double = int(math.log2(chunk)) - 1

    def stack(x):
        return jnp.concatenate([jnp.where(head0, x, 0.0), jnp.where(head0, 0.0, x)], axis=0)

    def hi_lo(x, axis):
        hi = x.astype(BF16)
        return jnp.concatenate([hi, (x - hi.astype(F32)).astype(BF16)], axis=axis)

    def seg_sums(xs):
        prods = [jnp.dot(hi_lo(x, 0), seg, preferred_element_type=F32) for x in xs]
        return [o[:chunk] + o[chunk:] for o in prods]

    lanes = V7X_LANES
    P = range(pairs)
    cols = [slice(p * lanes, (p + 1) * lanes) for p in P]
    zeros = jnp.zeros((c2, lanes), F32)

    def body(c, carry):
        rows = pl.ds(pl.multiple_of(c * chunk, chunk), chunk)
        ld = lambda ref: [ref[rows, cs].astype(F32) for cs in cols]
        par = lambda ref: [ref[:, cs] for cs in cols]
        r, lw, k, v, a = ld(r_ref), ld(lw_ref), ld(k_ref), ld(v_ref), ld(a_ref)
        k_k, k_a, r_k = par(kk_ref), par(ka_ref), par(rk_ref)

        kk = [k[p] * k_k[p] for p in P]
        ss = seg_sums([x * x for x in kk])
        kk = [kk[p] * lax.rsqrt(jnp.maximum(ss[p], 1e-24)) for p in P]
        k = [k[p] * (1.0 + (a[p] - 1.0) * k_a[p]) for p in P]
        b = [kk[p] * a[p] for p in P]
        bonus = seg_sums([r[p] * k[p] * r_k[p] for p in P])

        cc = [jnp.dot(ltri, hi_lo(lw[p], 1), preferred_element_type=F32) for p in P]
        cum = [x[:, :lanes] + x[:, lanes:] for x in cc]
        mid = [x[chunk // 2 - 1:chunk // 2] for x in cum]
        last = [x[chunk - 1:chunk] for x in cum]
        dec_mid = [jnp.exp(mid[p] - cum[p]) for p in P]
        dec_last = [jnp.exp(last[p] - cum[p]) for p in P]
        emid = [jnp.exp(x) for x in mid]
        plast = [jnp.exp(x) for x in last]
        As = [stack(-kk[p] * jnp.exp(cum[p] - lw[p] - mid[p])) for p in P]
        Rs = [stack(r[p] * jnp.exp(cum[p] - mid[p])) for p in P]
        Vs = [stack(v[p]) for p in P]
        AR = [jnp.concatenate([As[p], Rs[p]], axis=0) for p in P]
        BK = [jnp.concatenate([stack(b[p] * dec_mid[p]), stack(k[p] * dec_mid[p])], axis=0) for p in P]
        BKh = [jnp.concatenate([stack(b[p] * dec_last[p]), stack(k[p] * dec_last[p])], axis=0) for p in P]

        G = [_mm_nt(AR[p], BK[p]) for p in P]
        Aab = [jnp.where(strict, g[:c2, :c2], 0.0) for g in G]
        Aak = [jnp.where(strict, g[:c2, c2:], 0.0) for g in G]
        S = [jnp.concatenate([jnp.where(incl, g[c2:, :c2], 0.0), jnp.where(incl, g[c2:, c2:], 0.0)], axis=1)
             for g in G]

        X = [eye + x for x in Aab]
        Q = [_mm(x, x) for x in Aab]
        for _ in range(n_double - 1):
            QX = [_mm(Q[p], jnp.concatenate([Q[p], X[p]], axis=1)) for p in P]
            X = [X[p] + QX[p][:, lanes:] for p in P]
            Q = [x[:, :lanes] for x in QX]
        X = [X[p] + _mm(Q[p], X[p]) for p in P]

        AV = [_mm(Aak[p], Vs[p]) for p in P]
        TA = [_mm(X[p], jnp.concatenate([As[p], AV[p]], axis=1)) for p in P]
        lowV = [jnp.concatenate([zeros, Vs[p]], axis=1) for p in P]
        SY = [_mm(S[p], jnp.concatenate([TA[p], lowV[p]], axis=0)) for p in P]
        Rbar = [(Rs[p] + SY[p][:, :lanes]) * emid[p] for p in P]
        AU = [jnp.concatenate([TA[p][:, :lanes] * emid[p], TA[p][:, lanes:]], axis=1) for p in P]
        BKT = [jnp.transpose(x) for x in BKh]
        MN = [_mm(BKT[p], jnp.concatenate([AU[p], lowV[p]], axis=0)) for p in P]
        M = [eye * plast[p] + MN[p][:, :lanes] for p in P]
        H = [h_ref[p] for p in P]
        RH = [_mm(jnp.concatenate([Rbar[p], M[p]], axis=0), H[p]) for p in P]
        for p in P:
            h_ref[p] = RH[p][c2:] + MN[p][:, lanes:]
        Ys = [RH[p][:c2] + SY[p][:, lanes:] for p in P]
        y = [x[:chunk] + x[chunk:] for x in Ys]

        inv = 1.0 / RWKV_HEAD
        mean = seg_sums(y)
        yc = [y[p] - mean[p] * inv for p in P]
        var = seg_sums([x * x for x in yc])
        gn_g, gn_b = par(gg_ref), par(gb_ref)
        for p in P:
            yn = yc[p] * lax.rsqrt(var[p] * inv + GN_EPS) * gn_g[p] + gn_b[p]
            gate = g_ref[rows, cols[p]].astype(F32)
            o_ref[rows, cols[p]] = ((yn + bonus[p] * v[p]) * gate).astype(o_ref.dtype)
        return carry

    lax.fori_loop(0, r_ref.shape[0] // chunk, body, 0)


def wkv7(r, lw, k, v, a, g, k_k, k_a, r_k, gn_g, gn_b, batch, seq):
    n, d = r.shape
    pairs = min(WKV_PAIRS, d // V7X_LANES)
    width = pairs * V7X_LANES
    tb = _tile(seq, 256)
    seq_spec = pl.BlockSpec((tb, width), lambda b, p, t: (b * (seq // tb) + t, p))
    par_spec = pl.BlockSpec((1, width), lambda b, p, t: (0, p))
    row = lambda x: x.reshape(1, d).astype(F32)
    return pl.pallas_call(
        functools.partial(_wkv_kernel, chunk=WKV_CHUNK, pairs=pairs),
        grid=(batch, d // width, seq // tb),
        in_specs=[seq_spec] * 6 + [par_spec] * 5,
        out_specs=seq_spec,
        out_shape=jax.ShapeDtypeStruct((n, d), BF16),
        scratch_shapes=[pltpu.VMEM((pairs, V7X_LANES, V7X_LANES), F32)],
        compiler_params=_params("parallel", "parallel", "arbitrary"),
        name="wkv7",
    )(r, lw, k, v, a, g, row(k_k), row(k_a), row(r_k), row(gn_g), row(gn_b))


def _decay_epilogue(acc, w0):
    z = -(w0 + acc)
    softplus = jnp.maximum(z, 0.0) + jnp.log(1.0 + jnp.exp(-jnp.abs(z)))
    return -jnp.exp(-softplus - 0.5)


def _rwkv_prep_kernel(x_ref, prev_ref, g_ref, mix_ref, w1_ref, a1_ref, g1_ref, xs_ref, tw_ref, ta_ref, tg_ref,
                      *, tm, seq):
    i = pl.program_id(0)
    g = g_ref[...]
    xn = _rms(x_ref[...], g)
    prev = _rms(prev_ref[...], g)[V7X_SUBLANES - 1:V7X_SUBLANES]
    prev = jnp.where((i * tm) % seq == 0, 0.0, prev)
    row = lax.broadcasted_iota(jnp.int32, xn.shape, 0)
    xx = jnp.where(row == 0, prev, pltpu.roll(xn, 1, 0)) - xn
    mixed = lambda s: (xn + xx * mix_ref[s:s + 1, :]).astype(BF16)
    for slot, s in enumerate((0, 2, 3)):
        xs_ref[slot] = mixed(s)
    dot = lambda a, w_ref: jnp.dot(a, w_ref[...], preferred_element_type=F32)
    tw_ref[...] = jnp.tanh(dot(mixed(1), w1_ref)).astype(tw_ref.dtype)
    ta_ref[...] = dot(mixed(4), a1_ref).astype(ta_ref.dtype)
    tg_ref[...] = jax.nn.sigmoid(dot(mixed(5), g1_ref)).astype(tg_ref.dtype)


def rwkv_prep(h, norm_g, mix, w1, a1, g1, seq):
    n, d = h.shape
    tm = _tile(seq, 256)
    per = tm // V7X_SUBLANES
    narrow = [jnp.pad(w, ((0, 0), (0, -w.shape[1] % V7X_LANES))).astype(BF16) for w in (w1, a1, g1)]
    whole = lambda arr: pl.BlockSpec(arr.shape, lambda i: (0, 0))
    return pl.pallas_call(
        functools.partial(_rwkv_prep_kernel, tm=tm, seq=seq),
        grid=(n // tm,),
        in_specs=[pl.BlockSpec((tm, d), lambda i: (i, 0)),
                  pl.BlockSpec((V7X_SUBLANES, d), lambda i: (jnp.maximum(i * per - 1, 0), 0)),
                  pl.BlockSpec((1, d), lambda i: (0, 0)), whole(mix)] + [whole(w) for w in narrow],
        out_specs=[pl.BlockSpec((3, tm, d), lambda i: (0, i, 0))]
        + [pl.BlockSpec((tm, w.shape[1]), lambda i: (i, 0)) for w in narrow],
        out_shape=[jax.ShapeDtypeStruct((3, n, d), BF16)]
        + [jax.ShapeDtypeStruct((n, w.shape[1]), BF16) for w in narrow],
        compiler_params=_params("parallel"),
        name="rwkv_prep",
    )(h, h, norm_g.reshape(1, d), mix, *narrow)


def rwkv_layer(h, batch, seq, norm_g, mix, w_r, w_k, w_v, w0, w1, w2, a0, a1, a2, g1, g2, k_k, k_a, r_k, gn_g,
               gn_b, w_o):
    bf = lambda w: w.astype(BF16)
    pad_rows = lambda w: jnp.pad(w, ((0, -w.shape[0] % V7X_LANES), (0, 0)))
    xs, tw, ta, tg = rwkv_prep(h, norm_g, mix, w1, a1, g1, seq)
    r = matmul(xs, bf(w_r), x_sel=0, out_dtype=BF16, name="rwkv_r")
    k = matmul(xs, bf(w_k), x_sel=1, out_dtype=BF16, name="rwkv_k")
    v = matmul(xs, bf(w_v), x_sel=2, out_dtype=BF16, name="rwkv_v")
    lw = matmul(tw, bf(pad_rows(w2)), out_dtype=F32, rows=(w0,), name="rwkv_w2", epilogue=_decay_epilogue)
    a = matmul(ta, bf(pad_rows(a2)), out_dtype=BF16, rows=(a0,), name="rwkv_a2",
               epilogue=lambda acc, b: jax.nn.sigmoid(b + acc))
    g = matmul(tg, bf(pad_rows(g2)), out_dtype=BF16, name="rwkv_g2")
    y = wkv7(r, lw, k, v, a, g, k_k, k_a, r_k, gn_g, gn_b, batch, seq)
    return matmul(y, bf(w_o), out_dtype=F32, tiles=(h,), epilogue=lambda acc, res: res + acc, name="rwkv_o")


LOG2_E = 1.4426950408889634
ATTN_UNSHIFTED_MAX_LOG2 = 80.0
ATTN_UNROLL = 4
ATTN_LAG = 1
ATTN_SUM_ROWS = 16


def _sub_head_queries(q):
    lane = lax.broadcasted_iota(jnp.int32, q.shape, 1)
    zero = jnp.zeros_like(q)
    return jnp.where(lane < DIFF_HEAD, q, zero), jnp.where(lane < DIFF_HEAD, zero, q)


def _causal_tile(tq):
    return lax.broadcasted_iota(jnp.int32, (tq, tq), 0) >= lax.broadcasted_iota(jnp.int32, (tq, tq), 1)


def _attn_online_kernel(lam_ref, q_ref, k_ref, v_ref, sg_ref, o_ref, *, tq, out_scale):
    qi = pl.program_id(2)
    qs = _sub_head_queries(q_ref[...])
    causal = _causal_tile(tq)

    def step(j, carry, masked):
        start = pl.multiple_of(j * tq, tq)
        kj = k_ref[pl.ds(start, tq), :]
        vj = v_ref[pl.ds(start, tq), :]
        out = []
        for sub in range(2):
            m, l, acc = carry[sub]
            s = lax.dot_general(qs[sub], kj, (((1,), (1,)), ((), ())), preferred_element_type=F32)
            if masked:
                s = jnp.where(causal, s, -1e30)
            m_new = jnp.maximum(m, jnp.max(s, axis=-1, keepdims=True))
            alpha = jnp.exp2(m - m_new)
            p = jnp.exp2(s - m_new)
            l = alpha * l + jnp.sum(p, axis=-1, keepdims=True)
            acc = alpha * acc + jnp.dot(p.astype(BF16), vj, preferred_element_type=F32)
            out.append((m_new, l, acc))
        return tuple(out)

    init = tuple((jnp.full((tq, 1), -1e30, F32), jnp.zeros((tq, 1), F32), jnp.zeros((tq, V7X_LANES), F32))
                 for _ in range(2))
    carry = lax.fori_loop(0, qi, lambda j, c: step(j, c, False), init)
    (_, l0, acc0), (_, l1, acc1) = step(qi, carry, True)
    o = acc0 / l0 - lam_ref[0] * (acc1 / l1)
    o_ref[...] = (_rms(o, sg_ref[...]) * out_scale).astype(o_ref.dtype)


def _attn_unshifted_kernel(lam_ref, q_ref, k_ref, vt_ref, sg_ref, o_ref, acc_ref, *, tq, out_scale):
    qi = pl.program_id(2)
    qs = _sub_head_queries(q_ref[...])
    visible = lax.broadcasted_iota(jnp.int32, (tq, tq), 1) >= lax.broadcasted_iota(jnp.int32, (tq, tq), 0)
    ones = jnp.ones((ATTN_SUM_ROWS, tq), BF16)
    acc_ref[...] = jnp.zeros_like(acc_ref)

    def scores(j, sub, masked):
        kj = k_ref[pl.ds(pl.multiple_of(j * tq, tq), tq), :]
        st = lax.dot_general(kj, qs[sub], (((1,), (1,)), ((), ())), preferred_element_type=F32)
        return jnp.where(visible, st, -1e30) if masked else st

    def accumulate(j, sub, st):
        vt = jnp.concatenate([vt_ref[j], ones], axis=0)
        acc_ref[sub] += jnp.dot(vt, jnp.exp2(st).astype(BF16), preferred_element_type=F32)

    def steps(tiles):
        work = [(j, sub, masked) for j, masked in tiles for sub in range(2)]
        pending = []
        for j, sub, masked in work:
            pending.append((j, sub, scores(j, sub, masked)))
            if len(pending) > ATTN_LAG:
                accumulate(*pending.pop(0))
        for item in pending:
            accumulate(*item)

    def body(jj, c):
        steps([(jj * ATTN_UNROLL + u, False) for u in range(ATTN_UNROLL)])
        return c

    n_full = qi // ATTN_UNROLL
    lax.fori_loop(0, n_full, body, 0)
    first = n_full * ATTN_UNROLL
    for rem in range(ATTN_UNROLL):
        @pl.when(qi - first == rem)
        def _():
            steps([(first + u, False) for u in range(rem)] + [(qi, True)])
    a0 = acc_ref[0]
    a1 = acc_ref[1]
    hw = 2 * DIFF_HEAD
    ot = a0[:hw] / a0[hw:hw + 1] - lam_ref[0] * (a1[:hw] / a1[hw:hw + 1])
    ot = ot * lax.rsqrt(jnp.mean(ot * ot, axis=0, keepdims=True) + RMS_EPS)
    o_ref[...] = (jnp.transpose(ot) * (sg_ref[...] * out_scale)).astype(o_ref.dtype)


def diff_attention_core(q, k, v, lam, subln_g, batch, seq, lam_init, unshifted):
    n, d = q.shape
    tq = _tile(seq, 512)
    nq = seq // tq
    head_w = 2 * DIFF_HEAD
    q_spec = pl.BlockSpec((tq, head_w), lambda b, h, i: (b * nq + i, h))
    kv_spec = pl.BlockSpec((seq, head_w), lambda b, h, i: (b, h))
    if unshifted:
        body = _attn_unshifted_kernel
        scratch = [pltpu.VMEM((2, head_w + ATTN_SUM_ROWS, tq), F32)]
        v = jnp.transpose(v.reshape(n // tq, tq, d), (0, 2, 1))
        v_spec = pl.BlockSpec((nq, head_w, tq), lambda b, h, i: (b, h, 0))
    else:
        body = _attn_online_kernel
        scratch = []
        v_spec = kv_spec
    return pl.pallas_call(
        functools.partial(body, tq=tq, out_scale=1.0 - lam_init),
        grid=(batch, d // head_w, nq),
        in_specs=[pl.BlockSpec(memory_space=pltpu.SMEM), q_spec, kv_spec, v_spec,
                  pl.BlockSpec((1, head_w), lambda b, h, i: (0, 0))],
        out_specs=q_spec,
        out_shape=jax.ShapeDtypeStruct((n, d), BF16),
        scratch_shapes=scratch,
        compiler_params=_params("parallel", "parallel", "arbitrary"),
        name="diff_attn_unshifted" if unshifted else "diff_attn_online",
    )(lam.reshape(1).astype(F32), q, k, v, subln_g.reshape(1, head_w).astype(F32))


def shared_kv(h, kv_norm_g, w_kv, k_norm_g):
    d = h.shape[1]
    hn = rmsnorm(h, kv_norm_g)
    kg = jnp.tile(k_norm_g, d // DIFF_HEAD)
    w = w_kv.astype(BF16)
    k = matmul(hn, w, w_cols=(0, d), out_dtype=BF16, rows=(kg,), name="kv_k", tn=512,
               epilogue=functools.partial(_seg_rms_epilogue, seg=DIFF_HEAD, scale=1.0))
    v = matmul(hn, w, w_cols=(d, d), out_dtype=BF16, name="kv_v")
    return k, v


def attn_layer(h, k, v, batch, seq, layer_idx, norm_g, w_q, q_norm_g, k_norm_g, lq1, lk1, lq2, lk2, subln_g, w_o):
    d = h.shape[1]
    hn = rmsnorm(h, norm_g)
    qg = jnp.tile(q_norm_g, d // DIFF_HEAD)
    q_scale = LOG2_E * DIFF_HEAD ** -0.5
    q = matmul(hn, w_q.astype(BF16), out_dtype=BF16, rows=(qg,), name="attn_q", tn=512,
               epilogue=functools.partial(_seg_rms_epilogue, seg=DIFF_HEAD, scale=q_scale))
    lam_init = 0.8 - 0.6 * math.exp(-0.3 * layer_idx)
    lam = jnp.exp(jnp.sum(lq1 * lk1)) - jnp.exp(jnp.sum(lq2 * lk2)) + lam_init
    bound = 1.01 * q_scale * DIFF_HEAD * jnp.max(jnp.abs(q_norm_g)) * jnp.max(jnp.abs(k_norm_g))
    core = lambda unshifted: functools.partial(diff_attention_core, batch=batch, seq=seq, lam_init=lam_init,
                                               unshifted=unshifted)
    o = lax.cond(bound <= ATTN_UNSHIFTED_MAX_LOG2, core(True), core(False), q, k, v, lam, subln_g)
    return matmul(o, w_o.astype(BF16), out_dtype=F32, tiles=(h,), epilogue=lambda acc, res: res + acc,
                  name="attn_o")


PACK_ROWS = V7X_SUBLANES
ROUTE_ROWS = V7X_SUBLANES
DMA_LOOP_UNROLL = 8
DMA_PRIORITIES = 2
EXPERT_WEIGHT_CHUNK_BYTES = 1 << 20
EXPERT_STAGE_SLOTS = 4


def _pack_store(x, o_ref):
    m = x.shape[0]
    assert x.shape[1] == 2 * PACK_ROWS * V7X_LANES
    word = lambda j: pltpu.bitcast(x[:, j * V7X_LANES:(j + 1) * V7X_LANES].astype(BF16).astype(F32), jnp.uint32)
    for s in range(PACK_ROWS):
        o_ref[pl.ds(s, m, stride=PACK_ROWS), :] = word(s) | (word(s + PACK_ROWS) >> 16)


def _unpack_load(p_ref, m):
    words = [p_ref[pl.ds(s, m, stride=PACK_ROWS), :] for s in range(PACK_ROWS)]
    hi = [pltpu.bitcast(w & jnp.uint32(0xFFFF0000), F32) for w in words]
    lo = [pltpu.bitcast(w << 16, F32) for w in words]
    return jnp.concatenate(hi + lo, axis=1)


def _router_kernel(x_ref, g_ref, w_ref, b_ref, t_ref, route_ref, *, n_groups, per_group):
    t = _rms(x_ref[...], g_ref[...])
    _pack_store(t, t_ref)
    logits = jnp.dot(t, w_ref[...], preferred_element_type=F32, precision=lax.Precision.HIGHEST) + b_ref[...]
    lane = lax.broadcasted_iota(jnp.int32, logits.shape, 1)
    none = jnp.int32(logits.shape[1])
    neg = -jnp.inf
    rmax = lambda x: jnp.max(x, axis=-1, keepdims=True)
    first = lambda hit: jnp.min(jnp.where(hit, lane, none), axis=-1, keepdims=True)

    is_group = lane < n_groups
    g_logit = jnp.where(is_group, logits, neg)
    g_max = rmax(g_logit)
    grp = first(g_logit == g_max)
    p_group = 1.0 / jnp.sum(jnp.where(is_group, jnp.exp(logits - g_max), 0.0), axis=-1, keepdims=True)

    lo = n_groups + grp * per_group
    e_logit = jnp.where(jnp.logical_and(lane >= lo, lane < lo + per_group), logits, neg)
    top1 = rmax(e_logit)
    i1 = first(e_logit == top1)
    e_rest = jnp.where(lane == i1, neg, e_logit)
    top2 = rmax(e_rest)
    i2 = first(e_rest == top2)
    ratio = jnp.exp(top2 - top1)
    gate1 = p_group / (1.0 + ratio)
    gate2 = gate1 * ratio

    cols = [(i1 - n_groups).astype(F32), (i2 - n_groups).astype(F32), gate1, gate2]
    out = jnp.zeros_like(logits)
    for c, val in enumerate(cols):
        out = jnp.where(lane == c, val, out)
    route_ref[...] = jnp.transpose(out)[:ROUTE_ROWS]


def moe_router(h, norm_g, rg_w, rg_b, re_w, re_b):
    n, d = h.shape
    tm = _tile(n, 256)
    n_groups, n_exp = rg_w.shape[1], re_w.shape[1]
    assert n_groups + n_exp <= V7X_LANES
    pad = V7X_LANES - n_groups - n_exp
    w = jnp.pad(jnp.concatenate([rg_w, re_w], axis=1), ((0, 0), (0, pad)))
    b = jnp.pad(jnp.concatenate([rg_b, re_b]), (0, pad)).reshape(1, -1)
    return pl.pallas_call(
        functools.partial(_router_kernel, n_groups=n_groups, per_group=n_exp // n_groups),
        grid=(n // tm,),
        in_specs=[pl.BlockSpec((tm, d), lambda i: (i, 0)), pl.BlockSpec((1, d), lambda i: (0, 0)),
                  pl.BlockSpec((d, V7X_LANES), lambda i: (0, 0)), pl.BlockSpec((1, V7X_LANES), lambda i: (0, 0))],
        out_specs=[pl.BlockSpec((tm * PACK_ROWS, V7X_LANES), lambda i: (i, 0)),
                   pl.BlockSpec((ROUTE_ROWS, tm), lambda i: (0, i))],
        out_shape=[jax.ShapeDtypeStruct((n * PACK_ROWS, V7X_LANES), jnp.uint32),
                   jax.ShapeDtypeStruct((ROUTE_ROWS, n), F32)],
        compiler_params=_params("parallel"),
        name="moe_router",
    )(h, norm_g.reshape(1, d), w, b)


def _token_copy(src_hbm, token, dst, slot, sem):
    src = src_hbm.at[pl.ds(pl.multiple_of(token * PACK_ROWS, PACK_ROWS), PACK_ROWS), :]
    return pltpu.make_async_copy(src, dst.at[pl.ds(pl.multiple_of(slot * PACK_ROWS, PACK_ROWS), PACK_ROWS), :], sem)


def _expert_kernel(be_ref, nused_ref, idx_ref, idx_next_ref, x_hbm, wg_hbm, wu_hbm, wd_hbm, rw_ref, o_ref,
                   xbuf, wg_v, wu_v, wd_v, stage_in, stage_out, xsem, wsem, *, bm, layer, n_blocks):
    i = pl.program_id(0)
    slot = i % 2
    e = be_ref[i]
    used = i < nused_ref[0]
    new_expert = jnp.logical_or(i == 0, e != be_ref[jnp.maximum(i - 1, 0)])

    def gather(idx, s):
        for r in range(bm):
            _token_copy(x_hbm, idx[0, 0, r], xbuf.at[s], r, xsem.at[s]).start(priority=r % DMA_PRIORITIES)

    def drain(s):
        def body(r, c):
            _token_copy(x_hbm, 0, xbuf.at[s], r, xsem.at[s]).wait()
            return c
        lax.fori_loop(0, bm, body, 0, unroll=DMA_LOOP_UNROLL)

    def load(w_hbm, w_v, stage):
        slots, chunk = stage.shape[0], stage.shape[1]
        n_chunks = w_v.shape[0] // chunk
        copy = lambda c: pltpu.make_async_copy(w_hbm.at[layer, e, pl.ds(c * chunk, chunk), :], stage.at[c % slots],
                                               wsem.at[c % slots])
        for c in range(min(slots - 1, n_chunks)):
            copy(c).start()
        for c in range(n_chunks):
            if c + slots - 1 < n_chunks:
                copy(c + slots - 1).start()
            copy(c).wait()
            w_v[pl.ds(c * chunk, chunk), :] = stage[c % slots].astype(BF16)

    @pl.when(i == 0)
    def _():
        gather(idx_ref, 0)

    drain(slot)

    @pl.when(jnp.logical_and(used, new_expert))
    def _():
        load(wg_hbm, wg_v, stage_in)
        load(wu_hbm, wu_v, stage_in)
        load(wd_hbm, wd_v, stage_out)

    @pl.when(used)
    def _():
        gather(idx_next_ref, 1 - slot)
        x = _unpack_load(xbuf.at[slot], bm).astype(BF16)
        hg = jnp.dot(x, wg_v[...], preferred_element_type=F32)
        hu = jnp.dot(x, wu_v[...], preferred_element_type=F32)
        act = (hg * jax.nn.sigmoid(hg) * hu).astype(BF16)
        _pack_store(jnp.dot(act, wd_v[...], preferred_element_type=F32) * rw_ref[...], o_ref)

    @pl.when(jnp.logical_not(used))
    def _():
        gather(idx_next_ref, 1 - slot)
        o_ref[...] = jnp.zeros_like(o_ref)

    @pl.when(i == n_blocks - 1)
    def _():
        drain(1 - slot)


def moe_experts(x_packed, buf_tok, block_e, n_used, row_w, wg, wu, wd, layer, bm):
    n_pad = row_w.shape[0]
    n_blocks = n_pad // bm
    d, hid = wg.shape[2], wg.shape[3]
    chunk_in = EXPERT_WEIGHT_CHUNK_BYTES // (hid * 4)
    chunk_out = EXPERT_WEIGHT_CHUNK_BYTES // (d * 4)
    assert d % chunk_in == 0 and hid % chunk_out == 0
    packed = pl.BlockSpec((bm * PACK_ROWS, V7X_LANES), lambda i, be, nu: (i, 0))
    hbm = pl.BlockSpec(memory_space=pl.ANY)
    idx = buf_tok.reshape(n_blocks, 1, bm)
    return pl.pallas_call(
        functools.partial(_expert_kernel, bm=bm, layer=layer, n_blocks=n_blocks),
        grid_spec=pltpu.PrefetchScalarGridSpec(
            num_scalar_prefetch=2,
            grid=(n_blocks,),
            in_specs=[
                pl.BlockSpec((1, 1, bm), lambda i, be, nu: (i, 0, 0), memory_space=pltpu.SMEM),
                pl.BlockSpec((1, 1, bm), lambda i, be, nu: (jnp.minimum(i + 1, n_blocks - 1), 0, 0),
                             memory_space=pltpu.SMEM),
                hbm, hbm, hbm, hbm, pl.BlockSpec((bm, 1), lambda i, be, nu: (i, 0))],
            out_specs=packed,
            scratch_shapes=[
                pltpu.VMEM((2, bm * PACK_ROWS, V7X_LANES), jnp.uint32),
                pltpu.VMEM((d, hid), BF16), pltpu.VMEM((d, hid), BF16), pltpu.VMEM((hid, d), BF16),
                pltpu.VMEM((EXPERT_STAGE_SLOTS, chunk_in, hid), F32),
                pltpu.VMEM((EXPERT_STAGE_SLOTS, chunk_out, d), F32),
                pltpu.SemaphoreType.DMA((2,)), pltpu.SemaphoreType.DMA((EXPERT_STAGE_SLOTS,)),
            ],
        ),
        out_shape=jax.ShapeDtypeStruct((n_pad * PACK_ROWS, V7X_LANES), jnp.uint32),
        compiler_params=_params("arbitrary"),
        name="moe_experts",
    )(block_e, n_used, idx, idx, x_packed, wg, wu, wd, row_w.reshape(n_pad, 1))


def _combine_kernel(idx_ref, yb_hbm, h_ref, o_ref, buf, sem, *, rows):
    def issue(r, c):
        for s in range(TOP_K):
            _token_copy(yb_hbm, idx_ref[0, s, r], buf.at[s], r, sem).start(priority=s % DMA_PRIORITIES)
        return c

    def drain(r, c):
        for s in range(TOP_K):
            _token_copy(yb_hbm, 0, buf.at[s], r, sem).wait()
        return c

    lax.fori_loop(0, rows, issue, 0, unroll=DMA_LOOP_UNROLL)
    lax.fori_loop(0, rows, drain, 0, unroll=DMA_LOOP_UNROLL)
    acc = h_ref[...]
    for s in range(TOP_K):
        acc = acc + _unpack_load(buf.at[s], rows)
    o_ref[...] = acc


def moe_combine(h, yb, pos, rows_per_step=256):
    n, d = h.shape
    tc = _tile(n, rows_per_step)
    idx = pos.reshape(n // tc, tc, TOP_K).transpose(0, 2, 1)
    return pl.pallas_call(
        functools.partial(_combine_kernel, rows=tc),
        grid=(n // tc,),
        in_specs=[pl.BlockSpec((1, TOP_K, tc), lambda i: (i, 0, 0), memory_space=pltpu.SMEM),
                  pl.BlockSpec(memory_space=pl.ANY),
                  pl.BlockSpec((tc, d), lambda i: (i, 0))],
        out_specs=pl.BlockSpec((tc, d), lambda i: (i, 0)),
        out_shape=jax.ShapeDtypeStruct((n, d), F32),
        scratch_shapes=[pltpu.VMEM((TOP_K, tc * PACK_ROWS, V7X_LANES), jnp.uint32), pltpu.SemaphoreType.DMA(())],
        compiler_params=_params("arbitrary"),
        name="moe_combine",
    )(idx, yb, h)


def moe_layer(h, norm_g, rg_w, rg_b, re_w, re_b, wg, wu, wd, layer, bm=256):
    n, d = h.shape
    n_exp = wg.shape[1]
    t, route = moe_router(h, norm_g, rg_w, rg_b, re_w, re_b)
    flat_e = jnp.transpose(route[:TOP_K]).astype(jnp.int32).reshape(-1)
    gate = jnp.transpose(route[TOP_K:2 * TOP_K])

    n_assign = n * TOP_K
    n_pad = n_assign + n_exp * bm
    order = jnp.argsort(flat_e).astype(jnp.int32)
    rank = jnp.argsort(order).astype(jnp.int32)
    experts = jnp.arange(n_exp, dtype=jnp.int32)
    counts = jnp.sum(flat_e[None, :] == experts[:, None], axis=1, dtype=jnp.int32)
    padded = (counts + bm - 1) // bm * bm
    padded_end = jnp.cumsum(padded)
    padded_start = padded_end - padded
    start = jnp.cumsum(counts) - counts
    shift = padded_start - start
    pos = (rank + jnp.sum(jnp.where(flat_e[None, :] == experts[:, None], shift[:, None], 0), axis=0)).reshape(n, TOP_K)
    n_blocks = n_pad // bm
    block_start = jnp.arange(n_blocks, dtype=jnp.int32) * bm
    block_e = jnp.sum(block_start[:, None] >= padded_end[None, :], axis=1, dtype=jnp.int32)
    block_e = jnp.minimum(block_e, n_exp - 1)
    n_used = (padded_end[-1] // bm).astype(jnp.int32).reshape(1)
    per_row = lambda table: jnp.repeat(table[block_e], bm)
    off = jnp.arange(n_pad, dtype=jnp.int32) - per_row(padded_start)
    valid = off < per_row(counts)
    src = order[jnp.clip(per_row(start) + off, 0, n_assign - 1)]
    buf_tok = jnp.where(valid, src // TOP_K, 0)
    buf_w = jnp.where(valid, gate.reshape(-1)[src], 0.0)

    yb = moe_experts(t, buf_tok, block_e, n_used, buf_w, wg, wu, wd, layer, bm)
    return moe_combine(h, yb, pos)


def kernel(x, rwkv_norm_g, rwkv_mix, rwkv_w_r, rwkv_w_k, rwkv_w_v, rwkv_w0, rwkv_w1, rwkv_w2, rwkv_a0, rwkv_a1,
           rwkv_a2, rwkv_g1, rwkv_g2, rwkv_k_k, rwkv_k_a, rwkv_r_k, rwkv_gn_g, rwkv_gn_b, rwkv_w_o, kv_norm_g,
           w_kv, k_norm_g, attn_norm_g, attn_w_q, q_norm_g, lambda_q1, lambda_k1, lambda_q2, lambda_k2, subln_g,
           attn_w_o, moe_norm_g, router_group_w, router_group_b, router_expert_w, router_expert_b, expert_w_gate,
           expert_w_up, expert_w_down):
    batch, seq, d = x.shape
    depth = moe_norm_g.shape[0]
    n_a = rwkv_norm_g.shape[0]
    h = x.reshape(batch * seq, d)
    k_shared = v_shared = None
    for l in range(depth):
        if l < n_a:
            i = l
            h = rwkv_layer(h, batch, seq, rwkv_norm_g[i], rwkv_mix[i], rwkv_w_r[i], rwkv_w_k[i], rwkv_w_v[i],
                           rwkv_w0[i], rwkv_w1[i], rwkv_w2[i], rwkv_a0[i], rwkv_a1[i], rwkv_a2[i], rwkv_g1[i],
                           rwkv_g2[i], rwkv_k_k[i], rwkv_k_a[i], rwkv_r_k[i], rwkv_gn_g[i], rwkv_gn_b[i],
                           rwkv_w_o[i])
        else:
            j = l - n_a
            if j == 0:
                k_shared, v_shared = shared_kv(h, kv_norm_g, w_kv, k_norm_g)
            h = attn_layer(h, k_shared, v_shared, batch, seq, l, attn_norm_g[j], attn_w_q[j], q_norm_g[j], k_norm_g,
                           lambda_q1[j], lambda_k1[j], lambda_q2[j], lambda_k2[j], subln_g[j], attn_w_o[j])
        h = moe_layer(h, moe_norm_g[l], router_group_w[l], router_group_b[l], router_expert_w[l],
                      router_expert_b[l], expert_w_gate, expert_w_up, expert_w_down, l)
    return h.reshape(batch, seq, d)
```

```python
import functools
import math

import jax
import jax.numpy as jnp
from jax import lax
from jax.experimental import pallas as pl
from jax.experimental.pallas import tpu as pltpu

V7X_LANES = 128
V7X_SUBLANES = 8
V7X_VMEM_BYTES = 64 * 1024 * 1024
VMEM_LIMIT_BYTES = V7X_VMEM_BYTES - 8 * 1024 * 1024

RWKV_HEAD = 64
DIFF_HEAD = 64
TOP_K = 2
GN_EPS = 64e-5
RMS_EPS = 1e-6
WKV_CHUNK = 64
WKV_PAIRS = 16

BF16 = jnp.bfloat16
F32 = jnp.float32


def _tile(n, pref):
    t = min(n, pref)
    assert n % t == 0, (n, pref)
    return t


def _params(*sem):
    return pltpu.CompilerParams(dimension_semantics=sem, vmem_limit_bytes=VMEM_LIMIT_BYTES)


def _mm(a, b):
    return jnp.dot(a.astype(BF16), b.astype(BF16), preferred_element_type=F32)


def _mm_nt(a, b):
    return lax.dot_general(a.astype(BF16), b.astype(BF16), (((1,), (1,)), ((), ())), preferred_element_type=F32)


def _split_dot(x, w_bf16):
    hi = x.astype(BF16)
    lo = (x - hi.astype(F32)).astype(BF16)
    return (jnp.dot(hi, w_bf16, preferred_element_type=F32) + jnp.dot(lo, w_bf16, preferred_element_type=F32))


def _seg_ones(n, seg):
    i = lax.broadcasted_iota(jnp.int32, (n, n), 0) // seg
    j = lax.broadcasted_iota(jnp.int32, (n, n), 1) // seg
    return jnp.where(i == j, 1.0, 0.0).astype(BF16)


def _rms(x, g):
    return x * lax.rsqrt(jnp.mean(x * x, axis=-1, keepdims=True) + RMS_EPS) * g


def _matmul_kernel(x_ref, w_ref, *rest, epilogue, n_extra):
    extra = [r[...] for r in rest[:n_extra]]
    o_ref = rest[n_extra]
    acc = jnp.dot(x_ref[...], w_ref[...], preferred_element_type=F32)
    o_ref[...] = epilogue(acc, *extra).astype(o_ref.dtype)


def matmul(x, w, *, out_dtype, epilogue=None, rows=(), tiles=(), x_sel=None, w_cols=None, tm=1024, tn=1024,
           name="matmul"):
    if x_sel is None:
        m, k = x.shape
        x_spec = lambda tm_: pl.BlockSpec((tm_, k), lambda i, j: (i, 0))
    else:
        _, m, k = x.shape
        x_spec = lambda tm_: pl.BlockSpec((None, tm_, k), lambda i, j: (x_sel, i, 0))
    col0, n_out = (0, w.shape[1]) if w_cols is None else w_cols
    tm = _tile(m, tm)
    tn = _tile(n_out, tn)
    assert col0 % tn == 0
    j0 = col0 // tn
    if epilogue is None:
        epilogue = lambda acc: acc
    in_specs = [x_spec(tm), pl.BlockSpec((k, tn), lambda i, j: (0, j + j0))]
    in_specs += [pl.BlockSpec((1, tn), lambda i, j: (0, j)) for _ in rows]
    in_specs += [pl.BlockSpec((tm, tn), lambda i, j: (i, j)) for _ in tiles]
    return pl.pallas_call(
        functools.partial(_matmul_kernel, epilogue=epilogue, n_extra=len(rows) + len(tiles)),
        grid=(m // tm, n_out // tn),
        in_specs=in_specs,
        out_specs=pl.BlockSpec((tm, tn), lambda i, j: (i, j)),
        out_shape=jax.ShapeDtypeStruct((m, n_out), out_dtype),
        compiler_params=_params("parallel", "arbitrary"),
        name=name,
    )(x, w, *[r.reshape(1, n_out).astype(F32) for r in rows], *tiles)


def _norm_matmul_kernel(x_ref, g_ref, w_ref, *rest, epilogue, n_extra):
    extra = [r[...] for r in rest[:n_extra]]
    o_ref, xn_ref = rest[n_extra], rest[n_extra + 1]

    @pl.when(pl.program_id(1) == 0)
    def _():
        xn_ref[...] = _rms(x_ref[...], g_ref[...]).astype(xn_ref.dtype)

    acc = jnp.dot(xn_ref[...], w_ref[...], preferred_element_type=F32)
    o_ref[...] = epilogue(acc, *extra).astype(o_ref.dtype)


def norm_matmul(x, g, w, *, out_dtype, epilogue=None, rows=(), w_cols=None, tm=1024, tn=1024, name="norm_matmul"):
    m, k = x.shape
    col0, n_out = (0, w.shape[1]) if w_cols is None else w_cols
    tm = _tile(m, tm)
    tn = _tile(n_out, tn)
    assert col0 % tn == 0
    j0 = col0 // tn
    if epilogue is None:
        epilogue = lambda acc: acc
    in_specs = [pl.BlockSpec((tm, k), lambda i, j: (i, 0)), pl.BlockSpec((1, k), lambda i, j: (0, 0)),
                pl.BlockSpec((k, tn), lambda i, j: (0, j + j0))]
    in_specs += [pl.BlockSpec((1, tn), lambda i, j: (0, j)) for _ in rows]
    return pl.pallas_call(
        functools.partial(_norm_matmul_kernel, epilogue=epilogue, n_extra=len(rows)),
        grid=(m // tm, n_out // tn),
        in_specs=in_specs,
        out_specs=pl.BlockSpec((tm, tn), lambda i, j: (i, j)),
        out_shape=jax.ShapeDtypeStruct((m, n_out), out_dtype),
        scratch_shapes=[pltpu.VMEM((tm, k), BF16)],
        compiler_params=_params("parallel", "arbitrary"),
        name=name,
    )(x, g.reshape(1, k), w, *[r.reshape(1, n_out).astype(F32) for r in rows])


def _seg_rms_epilogue(acc, g, *, seg, scale):
    ones = _seg_ones(V7X_LANES, seg)
    outs = []
    for s in range(acc.shape[1] // V7X_LANES):
        a = acc[:, s * V7X_LANES:(s + 1) * V7X_LANES]
        ss = _split_dot(a * a, ones)
        outs.append(a * lax.rsqrt(ss * (1.0 / seg) + RMS_EPS))
    return jnp.concatenate(outs, axis=1) * (g * scale)


def _wkv_kernel(r_ref, lw_ref, k_ref, v_ref, a_ref, g_ref, kk_ref, ka_ref, rk_ref, gg_ref, gb_ref, o_ref, h_ref,
                *, chunk, pairs):
    c2 = 2 * chunk
    half = V7X_LANES // 2
    assert c2 == V7X_LANES, "two stacked heads of one chunk must fill one 128-row tile"

    @pl.when(pl.program_id(2) == 0)
    def _():
        h_ref[...] = jnp.zeros_like(h_ref)

    lane = lax.broadcasted_iota(jnp.int32, (chunk, V7X_LANES), 1)
    head0 = lane < half
    ri = lax.broadcasted_iota(jnp.int32, (c2, c2), 0)
    ci = lax.broadcasted_iota(jnp.int32, (c2, c2), 1)
    same = (ri // chunk) == (ci // chunk)
    strict = jnp.logical_and(same, (ci % chunk) < (ri % chunk))
    incl = jnp.logical_and(same, (ci % chunk) <= (ri % chunk))
    eye = jnp.where(ri == ci, 1.0, 0.0)
    li = lax.broadcasted_iota(jnp.int32, (chunk, chunk), 0)
    lj = lax.broadcasted_iota(jnp.int32, (chunk, chunk), 1)
    ltri = jnp.where(li >= lj, 1.0, 0.0).astype(BF16)
    seg = _seg_ones(V7X_LANES, RWKV_HEAD)
    n_double = int(math.log2(chunk)) - 1

    def stack(x):
        return jnp.concatenate([jnp.where(head0, x, 0.0), jnp.where(head0, 0.0, x)], axis=0)

    def hi_lo(x, axis):
        hi = x.astype(BF16)
        return jnp.concatenate([hi, (x - hi.astype(F32)).astype(BF16)], axis=axis)

    def seg_sums(xs):
        prods = [jnp.dot(hi_lo(x, 0), seg, preferred_element_type=F32) for x in xs]
        return [o[:chunk] + o[chunk:] for o in prods]

    lanes = V7X_LANES
    P = range(pairs)
    cols = [slice(p * lanes, (p + 1) * lanes) for p in P]
    zeros = jnp.zeros((c2, lanes), F32)

    def body(c, carry):
        rows = pl.ds(pl.multiple_of(c * chunk, chunk), chunk)
        ld = lambda ref: [ref[rows, cs].astype(F32) for cs in cols]
        par = lambda ref: [ref[:, cs] for cs in cols]
        r, lw, k, v, a = ld(r_ref), ld(lw_ref), ld(k_ref), ld(v_ref), ld(a_ref)
        k_k, k_a, r_k = par(kk_ref), par(ka_ref), par(rk_ref)

        kk = [k[p] * k_k[p] for p in P]
        ss = seg_sums([x * x for x in kk])
        kk = [kk[p] * lax.rsqrt(jnp.maximum(ss[p], 1e-24)) for p in P]
        k = [k[p] * (1.0 + (a[p] - 1.0) * k_a[p]) for p in P]
        b = [kk[p] * a[p] for p in P]
        bonus = seg_sums([r[p] * k[p] * r_k[p] for p in P])

        cc = [jnp.dot(ltri, hi_lo(lw[p], 1), preferred_element_type=F32) for p in P]
        cum = [x[:, :lanes] + x[:, lanes:] for x in cc]
        mid = [x[chunk // 2 - 1:chunk // 2] for x in cum]
        last = [x[chunk - 1:chunk] for x in cum]
        dec_mid = [jnp.exp(mid[p] - cum[p]) for p in P]
        dec_last = [jnp.exp(last[p] - cum[p]) for p in P]
        emid = [jnp.exp(x) for x in mid]
        plast = [jnp.exp(x) for x in last]
        As = [stack(-kk[p] * jnp.exp(cum[p] - lw[p] - mid[p])) for p in P]
        Rs = [stack(r[p] * jnp.exp(cum[p] - mid[p])) for p in P]
        Vs = [stack(v[p]) for p in P]
        AR = [jnp.concatenate([As[p], Rs[p]], axis=0) for p in P]
        BK = [jnp.concatenate([stack(b[p] * dec_mid[p]), stack(k[p] * dec_mid[p])], axis=0) for p in P]
        BKh = [jnp.concatenate([stack(b[p] * dec_last[p]), stack(k[p] * dec_last[p])], axis=0) for p in P]

        G = [_mm_nt(AR[p], BK[p]) for p in P]
        Aab = [jnp.where(strict, g[:c2, :c2], 0.0) for g in G]
        Aak = [jnp.where(strict, g[:c2, c2:], 0.0) for g in G]
        S = [jnp.concatenate([jnp.where(incl, g[c2:, :c2], 0.0), jnp.where(incl, g[c2:, c2:], 0.0)], axis=1)
             for g in G]

        X = [eye + x for x in Aab]
        Q = [_mm(x, x) for x in Aab]
        for _ in range(n_double - 1):
            QX = [_mm(Q[p], jnp.concatenate([Q[p], X[p]], axis=1)) for p in P]
            X = [X[p] + QX[p][:, lanes:] for p in P]
            Q = [x[:, :lanes] for x in QX]
        X = [X[p] + _mm(Q[p], X[p]) for p in P]

        AV = [_mm(Aak[p], Vs[p]) for p in P]
        TA = [_mm(X[p], jnp.concatenate([As[p], AV[p]], axis=1)) for p in P]
        lowV = [jnp.concatenate([zeros, Vs[p]], axis=1) for p in P]
        SY = [_mm(S[p], jnp.concatenate([TA[p], lowV[p]], axis=0)) for p in P]
        Rbar = [(Rs[p] + SY[p][:, :lanes]) * emid[p] for p in P]
        AU = [jnp.concatenate([TA[p][:, :lanes] * emid[p], TA[p][:, lanes:]], axis=1) for p in P]
        BKT = [jnp.transpose(x) for x in BKh]
        MN = [_mm(BKT[p], jnp.concatenate([AU[p], lowV[p]], axis=0)) for p in P]
        M = [eye * plast[p] + MN[p][:, :lanes] for p in P]
        H = [h_ref[p] for p in P]
        RH = [_mm(jnp.concatenate([Rbar[p], M[p]], axis=0), H[p]) for p in P]
        for p in P:
            h_ref[p] = RH[p][c2:] + MN[p][:, lanes:]
        Ys = [RH[p][:c2] + SY[p][:, lanes:] for p in P]
        y = [x[:chunk] + x[chunk:] for x in Ys]

        inv = 1.0 / RWKV_HEAD
        mean = seg_sums(y)
        yc = [y[p] - mean[p] * inv for p in P]
        var = seg_sums([x * x for x in yc])
        gn_g, gn_b = par(gg_ref), par(gb_ref)
        for p in P:
            yn = yc[p] * lax.rsqrt(var[p] * inv + GN_EPS) * gn_g[p] + gn_b[p]
            gate = g_ref[rows, cols[p]].astype(F32)
            o_ref[rows, cols[p]] = ((yn + bonus[p] * v[p]) * gate).astype(o_ref.dtype)
        return carry

    lax.fori_loop(0, r_ref.shape[0] // chunk, body, 0)


def wkv7(r, lw, k, v, a, g, k_k, k_a, r_k, gn_g, gn_b, batch, seq):
    n, d = r.shape
    pairs = min(WKV_PAIRS, d // V7X_LANES)
    width = pairs * V7X_LANES
    tb = _tile(seq, 256)
    seq_spec = pl.BlockSpec((tb, width), lambda b, p, t: (b * (seq // tb) + t, p))
    par_spec = pl.BlockSpec((1, width), lambda b, p, t: (0, p))
    row = lambda x: x.reshape(1, d).astype(F32)
    return pl.pallas_call(
        functools.partial(_wkv_kernel, chunk=WKV_CHUNK, pairs=pairs),
        grid=(batch, d // width, seq // tb),
        in_specs=[seq_spec] * 6 + [par_spec] * 5,
        out_specs=seq_spec,
        out_shape=jax.ShapeDtypeStruct((n, d), BF16),
        scratch_shapes=[pltpu.VMEM((pairs, V7X_LANES, V7X_LANES), F32)],
        compiler_params=_params("parallel", "parallel", "arbitrary"),
        name="wkv7",
    )(r, lw, k, v, a, g, row(k_k), row(k_a), row(r_k), row(gn_g), row(gn_b))


def _decay_epilogue(acc, w0):
    z = -(w0 + acc)
    softplus = jnp.maximum(z, 0.0) + jnp.log(1.0 + jnp.exp(-jnp.abs(z)))
    return -jnp.exp(-softplus - 0.5)


def _rwkv_prep_kernel(x_ref, prev_ref, g_ref, mix_ref, w1_ref, a1_ref, g1_ref, xs_ref, tw_ref, ta_ref, tg_ref,
                      *, tm, seq):
    i = pl.program_id(0)
    g = g_ref[...]
    xn = _rms(x_ref[...], g)
    prev = _rms(prev_ref[...], g)[V7X_SUBLANES - 1:V7X_SUBLANES]
    prev = jnp.where((i * tm) % seq == 0, 0.0, prev)
    row = lax.broadcasted_iota(jnp.int32, xn.shape, 0)
    xx = jnp.where(row == 0, prev, pltpu.roll(xn, 1, 0)) - xn
    mixed = lambda s: (xn + xx * mix_ref[s:s + 1, :]).astype(BF16)
    for slot, s in enumerate((0, 2, 3)):
        xs_ref[slot] = mixed(s)
    dot = lambda a, w_ref: jnp.dot(a, w_ref[...], preferred_element_type=F32)
    tw_ref[...] = jnp.tanh(dot(mixed(1), w1_ref)).astype(tw_ref.dtype)
    ta_ref[...] = dot(mixed(4), a1_ref).astype(ta_ref.dtype)
    tg_ref[...] = jax.nn.sigmoid(dot(mixed(5), g1_ref)).astype(tg_ref.dtype)


def rwkv_prep(h, norm_g, mix, w1, a1, g1, seq):
    n, d = h.shape
    tm = _tile(seq, 256)
    per = tm // V7X_SUBLANES
    narrow = [jnp.pad(w, ((0, 0), (0, -w.shape[1] % V7X_LANES))).astype(BF16) for w in (w1, a1, g1)]
    whole = lambda arr: pl.BlockSpec(arr.shape, lambda i: (0, 0))
    return pl.pallas_call(
        functools.partial(_rwkv_prep_kernel, tm=tm, seq=seq),
        grid=(n // tm,),
        in_specs=[pl.BlockSpec((tm, d), lambda i: (i, 0)),
                  pl.BlockSpec((V7X_SUBLANES, d), lambda i: (jnp.maximum(i * per - 1, 0), 0)),
                  pl.BlockSpec((1, d), lambda i: (0, 0)), whole(mix)] + [whole(w) for w in narrow],
        out_specs=[pl.BlockSpec((3, tm, d), lambda i: (0, i, 0))]
        + [pl.BlockSpec((tm, w.shape[1]), lambda i: (i, 0)) for w in narrow],
        out_shape=[jax.ShapeDtypeStruct((3, n, d), BF16)]
        + [jax.ShapeDtypeStruct((n, w.shape[1]), BF16) for w in narrow],
        compiler_params=_params("parallel"),
        name="rwkv_prep",
    )(h, h, norm_g.reshape(1, d), mix, *narrow)


def rwkv_layer(h, batch, seq, norm_g, mix, w_r, w_k, w_v, w0, w1, w2, a0, a1, a2, g1, g2, k_k, k_a, r_k, gn_g,
               gn_b, w_o):
    bf = lambda w: w.astype(BF16)
    pad_rows = lambda w: jnp.pad(w, ((0, -w.shape[0] % V7X_LANES), (0, 0)))
    xs, tw, ta, tg = rwkv_prep(h, norm_g, mix, w1, a1, g1, seq)
    r = matmul(xs, bf(w_r), x_sel=0, out_dtype=BF16, name="rwkv_r")
    k = matmul(xs, bf(w_k), x_sel=1, out_dtype=BF16, name="rwkv_k")
    v = matmul(xs, bf(w_v), x_sel=2, out_dtype=BF16, name="rwkv_v")
    lw = matmul(tw, bf(pad_rows(w2)), out_dtype=F32, rows=(w0,), name="rwkv_w2", epilogue=_decay_epilogue)
    a = matmul(ta, bf(pad_rows(a2)), out_dtype=BF16, rows=(a0,), name="rwkv_a2",
               epilogue=lambda acc, b: jax.nn.sigmoid(b + acc))
    g = matmul(tg, bf(pad_rows(g2)), out_dtype=BF16, name="rwkv_g2")
    y = wkv7(r, lw, k, v, a, g, k_k, k_a, r_k, gn_g, gn_b, batch, seq)
    return matmul(y, bf(w_o), out_dtype=F32, tiles=(h,), epilogue=lambda acc, res: res + acc, name="rwkv_o")


LOG2_E = 1.4426950408889634
ATTN_UNSHIFTED_MAX_LOG2 = 80.0
ATTN_UNROLL = 4
ATTN_LAG = 1
ATTN_SUM_ROWS = 16


def _sub_head_queries(q):
    lane = lax.broadcasted_iota(jnp.int32, q.shape, 1)
    zero = jnp.zeros_like(q)
    return jnp.where(lane < DIFF_HEAD, q, zero), jnp.where(lane < DIFF_HEAD, zero, q)


def _causal_tile(tq):
    return lax.broadcasted_iota(jnp.int32, (tq, tq), 0) >= lax.broadcasted_iota(jnp.int32, (tq, tq), 1)


def _attn_online_kernel(lam_ref, q_ref, k_ref, v_ref, sg_ref, o_ref, *, tq, out_scale):
    qi = pl.program_id(2)
    qs = _sub_head_queries(q_ref[...])
    causal = _causal_tile(tq)

    def step(j, carry, masked):
        start = pl.multiple_of(j * tq, tq)
        kj = k_ref[pl.ds(start, tq), :]
        vj = v_ref[pl.ds(start, tq), :]
        out = []
        for sub in range(2):
            m, l, acc = carry[sub]
            s = lax.dot_general(qs[sub], kj, (((1,), (1,)), ((), ())), preferred_element_type=F32)
            if masked:
                s = jnp.where(causal, s, -1e30)
            m_new = jnp.maximum(m, jnp.max(s, axis=-1, keepdims=True))
            alpha = jnp.exp2(m - m_new)
            p = jnp.exp2(s - m_new)
            l = alpha * l + jnp.sum(p, axis=-1, keepdims=True)
            acc = alpha * acc + jnp.dot(p.astype(BF16), vj, preferred_element_type=F32)
            out.append((m_new, l, acc))
        return tuple(out)

    init = tuple((jnp.full((tq, 1), -1e30, F32), jnp.zeros((tq, 1), F32), jnp.zeros((tq, V7X_LANES), F32))
                 for _ in range(2))
    carry = lax.fori_loop(0, qi, lambda j, c: step(j, c, False), init)
    (_, l0, acc0), (_, l1, acc1) = step(qi, carry, True)
    o = acc0 / l0 - lam_ref[0] * (acc1 / l1)
    o_ref[...] = (_rms(o, sg_ref[...]) * out_scale).astype(o_ref.dtype)


def _attn_unshifted_kernel(lam_ref, q_ref, k_ref, vt_ref, sg_ref, o_ref, acc_ref, *, tq, out_scale):
    qi = pl.program_id(2)
    qs = _sub_head_queries(q_ref[...])
    visible = lax.broadcasted_iota(jnp.int32, (tq, tq), 1) >= lax.broadcasted_iota(jnp.int32, (tq, tq), 0)
    ones = jnp.ones((ATTN_SUM_ROWS, tq), BF16)
    acc_ref[...] = jnp.zeros_like(acc_ref)

    def scores(j, sub, masked):
        kj = k_ref[pl.ds(pl.multiple_of(j * tq, tq), tq), :]
        st = lax.dot_general(kj, qs[sub], (((1,), (1,)), ((), ())), preferred_element_type=F32)
        return jnp.where(visible, st, -1e30) if masked else st

    def accumulate(j, sub, st):
        vt = jnp.concatenate([vt_ref[j], ones], axis=0)
        acc_ref[sub] += jnp.dot(vt, jnp.exp2(st).astype(BF16), preferred_element_type=F32)

    def steps(tiles):
        work = [(j, sub, masked) for j, masked in tiles for sub in range(2)]
        pending = []
        for j, sub, masked in work:
            pending.append((j, sub, scores(j, sub, masked)))
            if len(pending) > ATTN_LAG:
                accumulate(*pending.pop(0))
        for item in pending:
            accumulate(*item)

    def body(jj, c):
        steps([(jj * ATTN_UNROLL + u, False) for u in range(ATTN_UNROLL)])
        return c

    n_full = qi // ATTN_UNROLL
    lax.fori_loop(0, n_full, body, 0)
    first = n_full * ATTN_UNROLL
    for rem in range(ATTN_UNROLL):
        @pl.when(qi - first == rem)
        def _():
            steps([(first + u, False) for u in range(rem)] + [(qi, True)])
    a0 = acc_ref[0]
    a1 = acc_ref[1]
    hw = 2 * DIFF_HEAD
    ot = a0[:hw] / a0[hw:hw + 1] - lam_ref[0] * (a1[:hw] / a1[hw:hw + 1])
    ot = ot * lax.rsqrt(jnp.mean(ot * ot, axis=0, keepdims=True) + RMS_EPS)
    o_ref[...] = (jnp.transpose(ot) * (sg_ref[...] * out_scale)).astype(o_ref.dtype)


def diff_attention_core(q, k, v, lam, subln_g, batch, seq, lam_init, unshifted):
    n, d = q.shape
    tq = _tile(seq, 512)
    nq = seq // tq
    head_w = 2 * DIFF_HEAD
    q_spec = pl.BlockSpec((tq, head_w), lambda b, h, i: (b * nq + i, h))
    kv_spec = pl.BlockSpec((seq, head_w), lambda b, h, i: (b, h))
    if unshifted:
        body = _attn_unshifted_kernel
        scratch = [pltpu.VMEM((2, head_w + ATTN_SUM_ROWS, tq), F32)]
        v = jnp.transpose(v.reshape(n // tq, tq, d), (0, 2, 1))
        v_spec = pl.BlockSpec((nq, head_w, tq), lambda b, h, i: (b, h, 0))
    else:
        body = _attn_online_kernel
        scratch = []
        v_spec = kv_spec
    return pl.pallas_call(
        functools.partial(body, tq=tq, out_scale=1.0 - lam_init),
        grid=(batch, d // head_w, nq),
        in_specs=[pl.BlockSpec(memory_space=pltpu.SMEM), q_spec, kv_spec, v_spec,
                  pl.BlockSpec((1, head_w), lambda b, h, i: (0, 0))],
        out_specs=q_spec,
        out_shape=jax.ShapeDtypeStruct((n, d), BF16),
        scratch_shapes=scratch,
        compiler_params=_params("parallel", "parallel", "arbitrary"),
        name="diff_attn_unshifted" if unshifted else "diff_attn_online",
    )(lam.reshape(1).astype(F32), q, k, v, subln_g.reshape(1, head_w).astype(F32))


def shared_kv(h, kv_norm_g, w_kv, k_norm_g):
    d = h.shape[1]
    kg = jnp.tile(k_norm_g, d // DIFF_HEAD)
    w = w_kv.astype(BF16)
    k = norm_matmul(h, kv_norm_g, w, w_cols=(0, d), out_dtype=BF16, rows=(kg,), name="kv_k", tn=512,
                    epilogue=functools.partial(_seg_rms_epilogue, seg=DIFF_HEAD, scale=1.0))
    v = norm_matmul(h, kv_norm_g, w, w_cols=(d, d), out_dtype=BF16, name="kv_v")
    return k, v


def attn_layer(h, k, v, batch, seq, layer_idx, norm_g, w_q, q_norm_g, k_norm_g, lq1, lk1, lq2, lk2, subln_g, w_o):
    d = h.shape[1]
    qg = jnp.tile(q_norm_g, d // DIFF_HEAD)
    q_scale = LOG2_E * DIFF_HEAD ** -0.5
    q = norm_matmul(h, norm_g, w_q.astype(BF16), out_dtype=BF16, rows=(qg,), name="attn_q", tn=512,
                    epilogue=functools.partial(_seg_rms_epilogue, seg=DIFF_HEAD, scale=q_scale))
    lam_init = 0.8 - 0.6 * math.exp(-0.3 * layer_idx)
    lam = jnp.exp(jnp.sum(lq1 * lk1)) - jnp.exp(jnp.sum(lq2 * lk2)) + lam_init
    bound = 1.01 * q_scale * DIFF_HEAD * jnp.max(jnp.abs(q_norm_g)) * jnp.max(jnp.abs(k_norm_g))
    core = lambda unshifted: functools.partial(diff_attention_core, batch=batch, seq=seq, lam_init=lam_init,
                                               unshifted=unshifted)
    o = lax.cond(bound <= ATTN_UNSHIFTED_MAX_LOG2, core(True), core(False), q, k, v, lam, subln_g)
    return matmul(o, w_o.astype(BF16), out_dtype=F32, tiles=(h,), epilogue=lambda acc, res: res + acc,
                  name="attn_o")


PACK_ROWS = V7X_SUBLANES
ROUTE_ROWS = V7X_SUBLANES
DMA_LOOP_UNROLL = 8
DMA_PRIORITIES = 2
EXPERT_WEIGHT_CHUNK_BYTES = 1 << 20
EXPERT_STAGE_SLOTS = 4


def _pack_store(x, o_ref):
    m = x.shape[0]
    assert x.shape[1] == 2 * PACK_ROWS * V7X_LANES
    word = lambda j: pltpu.bitcast(x[:, j * V7X_LANES:(j + 1) * V7X_LANES].astype(BF16).astype(F32), jnp.uint32)
    for s in range(PACK_ROWS):
        o_ref[pl.ds(s, m, stride=PACK_ROWS), :] = word(s) | (word(s + PACK_ROWS) >> 16)


def _unpack_load(p_ref, m):
    words = [p_ref[pl.ds(s, m, stride=PACK_ROWS), :] for s in range(PACK_ROWS)]
    hi = [pltpu.bitcast(w & jnp.uint32(0xFFFF0000), F32) for w in words]
    lo = [pltpu.bitcast(w << 16, F32) for w in words]
    return jnp.concatenate(hi + lo, axis=1)


def _router_kernel(x_ref, g_ref, w_ref, b_ref, t_ref, route_ref, *, n_groups, per_group):
    t = _rms(x_ref[...], g_ref[...])
    _pack_store(t, t_ref)
    logits = jnp.dot(t, w_ref[...], preferred_element_type=F32, precision=lax.Precision.HIGHEST) + b_ref[...]
    lane = lax.broadcasted_iota(jnp.int32, logits.shape, 1)
    none = jnp.int32(logits.shape[1])
    neg = -jnp.inf
    rmax = lambda x: jnp.max(x, axis=-1, keepdims=True)
    first = lambda hit: jnp.min(jnp.where(hit, lane, none), axis=-1, keepdims=True)

    is_group = lane < n_groups
    g_logit = jnp.where(is_group, logits, neg)
    g_max = rmax(g_logit)
    grp = first(g_logit == g_max)
    p_group = 1.0 / jnp.sum(jnp.where(is_group, jnp.exp(logits - g_max), 0.0), axis=-1, keepdims=True)

    lo = n_groups + grp * per_group
    e_logit = jnp.where(jnp.logical_and(lane >= lo, lane < lo + per_group), logits, neg)
    top1 = rmax(e_logit)
    i1 = first(e_logit == top1)
    e_rest = jnp.where(lane == i1, neg, e_logit)
    top2 = rmax(e_rest)
    i2 = first(e_rest == top2)
    ratio = jnp.exp(top2 - top1)
    gate1 = p_group / (1.0 + ratio)
    gate2 = gate1 * ratio

    cols = [(i1 - n_groups).astype(F32), (i2 - n_groups).astype(F32), gate1, gate2]
    out = jnp.zeros_like(logits)
    for c, val in enumerate(cols):
        out = jnp.where(lane == c, val, out)
    route_ref[...] = jnp.transpose(out)[:ROUTE_ROWS]


def moe_router(h, norm_g, rg_w, rg_b, re_w, re_b):
    n, d = h.shape
    tm = _tile(n, 256)
    n_groups, n_exp = rg_w.shape[1], re_w.shape[1]
    assert n_groups + n_exp <= V7X_LANES
    pad = V7X_LANES - n_groups - n_exp
    w = jnp.pad(jnp.concatenate([rg_w, re_w], axis=1), ((0, 0), (0, pad)))
    b = jnp.pad(jnp.concatenate([rg_b, re_b]), (0, pad)).reshape(1, -1)
    return pl.pallas_call(
        functools.partial(_router_kernel, n_groups=n_groups, per_group=n_exp // n_groups),
        grid=(n // tm,),
        in_specs=[pl.BlockSpec((tm, d), lambda i: (i, 0)), pl.BlockSpec((1, d), lambda i: (0, 0)),
                  pl.BlockSpec((d, V7X_LANES), lambda i: (0, 0)), pl.BlockSpec((1, V7X_LANES), lambda i: (0, 0))],
        out_specs=[pl.BlockSpec((tm * PACK_ROWS, V7X_LANES), lambda i: (i, 0)),
                   pl.BlockSpec((ROUTE_ROWS, tm), lambda i: (0, i))],
        out_shape=[jax.ShapeDtypeStruct((n * PACK_ROWS, V7X_LANES), jnp.uint32),
                   jax.ShapeDtypeStruct((ROUTE_ROWS, n), F32)],
        compiler_params=_params("parallel"),
        name="moe_router",
    )(h, norm_g.reshape(1, d), w, b)


def _token_copy(src_hbm, token, dst, slot, sem):
    src = src_hbm.at[pl.ds(pl.multiple_of(token * PACK_ROWS, PACK_ROWS), PACK_ROWS), :]
    return pltpu.make_async_copy(src, dst.at[pl.ds(pl.multiple_of(slot * PACK_ROWS, PACK_ROWS), PACK_ROWS), :], sem)


def _expert_kernel(be_ref, nused_ref, idx_ref, idx_next_ref, x_hbm, wg_hbm, wu_hbm, wd_hbm, rw_ref, o_ref,
                   xbuf, wg_v, wu_v, wd_v, stage_in, stage_out, xsem, wsem, *, bm, layer, n_blocks):
    i = pl.program_id(0)
    slot = i % 2
    e = be_ref[i]
    used = i < nused_ref[0]
    new_expert = jnp.logical_or(i == 0, e != be_ref[jnp.maximum(i - 1, 0)])

    def gather(idx, s):
        for r in range(bm):
            _token_copy(x_hbm, idx[0, 0, r], xbuf.at[s], r, xsem.at[s]).start(priority=r % DMA_PRIORITIES)

    def drain(s):
        def body(r, c):
            _token_copy(x_hbm, 0, xbuf.at[s], r, xsem.at[s]).wait()
            return c
        lax.fori_loop(0, bm, body, 0, unroll=DMA_LOOP_UNROLL)

    def load(w_hbm, w_v, stage):
        slots, chunk = stage.shape[0], stage.shape[1]
        n_chunks = w_v.shape[0] // chunk
        copy = lambda c: pltpu.make_async_copy(w_hbm.at[layer, e, pl.ds(c * chunk, chunk), :], stage.at[c % slots],
                                               wsem.at[c % slots])
        for c in range(min(slots - 1, n_chunks)):
            copy(c).start()
        for c in range(n_chunks):
            if c + slots - 1 < n_chunks:
                copy(c + slots - 1).start()
            copy(c).wait()
            w_v[pl.ds(c * chunk, chunk), :] = stage[c % slots].astype(BF16)

    @pl.when(i == 0)
    def _():
        gather(idx_ref, 0)

    drain(slot)

    @pl.when(jnp.logical_and(used, new_expert))
    def _():
        load(wg_hbm, wg_v, stage_in)
        load(wu_hbm, wu_v, stage_in)
        load(wd_hbm, wd_v, stage_out)

    @pl.when(used)
    def _():
        gather(idx_next_ref, 1 - slot)
        x = _unpack_load(xbuf.at[slot], bm).astype(BF16)
        hg = jnp.dot(x, wg_v[...], preferred_element_type=F32)
        hu = jnp.dot(x, wu_v[...], preferred_element_type=F32)
        act = (hg * jax.nn.sigmoid(hg) * hu).astype(BF16)
        _pack_store(jnp.dot(act, wd_v[...], preferred_element_type=F32) * rw_ref[...], o_ref)

    @pl.when(jnp.logical_not(used))
    def _():
        gather(idx_next_ref, 1 - slot)
        o_ref[...] = jnp.zeros_like(o_ref)

    @pl.when(i == n_blocks - 1)
    def _():
        drain(1 - slot)


def moe_experts(x_packed, buf_tok, block_e, n_used, row_w, wg, wu, wd, layer, bm):
    n_pad = row_w.shape[0]
    n_blocks = n_pad // bm
    d, hid = wg.shape[2], wg.shape[3]
    chunk_in = EXPERT_WEIGHT_CHUNK_BYTES // (hid * 4)
    chunk_out = EXPERT_WEIGHT_CHUNK_BYTES // (d * 4)
    assert d % chunk_in == 0 and hid % chunk_out == 0
    packed = pl.BlockSpec((bm * PACK_ROWS, V7X_LANES), lambda i, be, nu: (i, 0))
    hbm = pl.BlockSpec(memory_space=pl.ANY)
    idx = buf_tok.reshape(n_blocks, 1, bm)
    return pl.pallas_call(
        functools.partial(_expert_kernel, bm=bm, layer=layer, n_blocks=n_blocks),
        grid_spec=pltpu.PrefetchScalarGridSpec(
            num_scalar_prefetch=2,
            grid=(n_blocks,),
            in_specs=[
                pl.BlockSpec((1, 1, bm), lambda i, be, nu: (i, 0, 0), memory_space=pltpu.SMEM),
                pl.BlockSpec((1, 1, bm), lambda i, be, nu: (jnp.minimum(i + 1, n_blocks - 1), 0, 0),
                             memory_space=pltpu.SMEM),
                hbm, hbm, hbm, hbm, pl.BlockSpec((bm, 1), lambda i, be, nu: (i, 0))],
            out_specs=packed,
            scratch_shapes=[
                pltpu.VMEM((2, bm * PACK_ROWS, V7X_LANES), jnp.uint32),
                pltpu.VMEM((d, hid), BF16), pltpu.VMEM((d, hid), BF16), pltpu.VMEM((hid, d), BF16),
                pltpu.VMEM((EXPERT_STAGE_SLOTS, chunk_in, hid), F32),
                pltpu.VMEM((EXPERT_STAGE_SLOTS, chunk_out, d), F32),
                pltpu.SemaphoreType.DMA((2,)), pltpu.SemaphoreType.DMA((EXPERT_STAGE_SLOTS,)),
            ],
        ),
        out_shape=jax.ShapeDtypeStruct((n_pad * PACK_ROWS, V7X_LANES), jnp.uint32),
        compiler_params=_params("arbitrary"),
        name="moe_experts",
    )(block_e, n_used, idx, idx, x_packed, wg, wu, wd, row_w.reshape(n_pad, 1))


def _combine_kernel(idx_ref, yb_hbm, h_ref, o_ref, buf, sem, *, rows):
    def issue(r, c):
        for s in range(TOP_K):
            _token_copy(yb_hbm, idx_ref[0, s, r], buf.at[s], r, sem).start(priority=s % DMA_PRIORITIES)
        return c

    def drain(r, c):
        for s in range(TOP_K):
            _token_copy(yb_hbm, 0, buf.at[s], r, sem).wait()
        return c

    lax.fori_loop(0, rows, issue, 0, unroll=DMA_LOOP_UNROLL)
    lax.fori_loop(0, rows, drain, 0, unroll=DMA_LOOP_UNROLL)
    acc = h_ref[...]
    for s in range(TOP_K):
        acc = acc + _unpack_load(buf.at[s], rows)
    o_ref[...] = acc


def moe_combine(h, yb, pos, rows_per_step=256):
    n, d = h.shape
    tc = _tile(n, rows_per_step)
    idx = pos.reshape(n // tc, tc, TOP_K).transpose(0, 2, 1)
    return pl.pallas_call(
        functools.partial(_combine_kernel, rows=tc),
        grid=(n // tc,),
        in_specs=[pl.BlockSpec((1, TOP_K, tc), lambda i: (i, 0, 0), memory_space=pltpu.SMEM),
                  pl.BlockSpec(memory_space=pl.ANY),
                  pl.BlockSpec((tc, d), lambda i: (i, 0))],
        out_specs=pl.BlockSpec((tc, d), lambda i: (i, 0)),
        out_shape=jax.ShapeDtypeStruct((n, d), F32),
        scratch_shapes=[pltpu.VMEM((TOP_K, tc * PACK_ROWS, V7X_LANES), jnp.uint32), pltpu.SemaphoreType.DMA(())],
        compiler_params=_params("arbitrary"),
        name="moe_combine",
    )(idx, yb, h)


def moe_layer(h, norm_g, rg_w, rg_b, re_w, re_b, wg, wu, wd, layer, bm=256):
    n, d = h.shape
    n_exp = wg.shape[1]
    t, route = moe_router(h, norm_g, rg_w, rg_b, re_w, re_b)
    flat_e = jnp.transpose(route[:TOP_K]).astype(jnp.int32).reshape(-1)
    gate = jnp.transpose(route[TOP_K:2 * TOP_K])

    n_assign = n * TOP_K
    n_pad = n_assign + n_exp * bm
    order = jnp.argsort(flat_e).astype(jnp.int32)
    rank = jnp.argsort(order).astype(jnp.int32)
    experts = jnp.arange(n_exp, dtype=jnp.int32)
    counts = jnp.sum(flat_e[None, :] == experts[:, None], axis=1, dtype=jnp.int32)
    padded = (counts + bm - 1) // bm * bm
    padded_end = jnp.cumsum(padded)
    padded_start = padded_end - padded
    start = jnp.cumsum(counts) - counts
    shift = padded_start - start
    pos = (rank + jnp.sum(jnp.where(flat_e[None, :] == experts[:, None], shift[:, None], 0), axis=0)).reshape(n, TOP_K)
    n_blocks = n_pad // bm
    block_start = jnp.arange(n_blocks, dtype=jnp.int32) * bm
    block_e = jnp.sum(block_start[:, None] >= padded_end[None, :], axis=1, dtype=jnp.int32)
    block_e = jnp.minimum(block_e, n_exp - 1)
    n_used = (padded_end[-1] // bm).astype(jnp.int32).reshape(1)
    per_row = lambda table: jnp.repeat(table[block_e], bm)
    off = jnp.arange(n_pad, dtype=jnp.int32) - per_row(padded_start)
    valid = off < per_row(counts)
    src = order[jnp.clip(per_row(start) + off, 0, n_assign - 1)]
    buf_tok = jnp.where(valid, src // TOP_K, 0)
    buf_w = jnp.where(valid, gate.reshape(-1)[src], 0.0)

    yb = moe_experts(t, buf_tok, block_e, n_used, buf_w, wg, wu, wd, layer, bm)
    return moe_combine(h, yb, pos)


def kernel(x, rwkv_norm_g, rwkv_mix, rwkv_w_r, rwkv_w_k, rwkv_w_v, rwkv_w0, rwkv_w1, rwkv_w2, rwkv_a0, rwkv_a1,
           rwkv_a2, rwkv_g1, rwkv_g2, rwkv_k_k, rwkv_k_a, rwkv_r_k, rwkv_gn_g, rwkv_gn_b, rwkv_w_o, kv_norm_g,
           w_kv, k_norm_g, attn_norm_g, attn_w_q, q_norm_g, lambda_q1, lambda_k1, lambda_q2, lambda_k2, subln_g,
           attn_w_o, moe_norm_g, router_group_w, router_group_b, router_expert_w, router_expert_b, expert_w_gate,
           expert_w_up, expert_w_down):
    batch, seq, d = x.shape
    depth = moe_norm_g.shape[0]
    n_a = rwkv_norm_g.shape[0]
    h = x.reshape(batch * seq, d)
    k_shared = v_shared = None
    for l in range(depth):
        if l < n_a:
            i = l
            h = rwkv_layer(h, batch, seq, rwkv_norm_g[i], rwkv_mix[i], rwkv_w_r[i], rwkv_w_k[i], rwkv_w_v[i],
                           rwkv_w0[i], rwkv_w1[i], rwkv_w2[i], rwkv_a0[i], rwkv_a1[i], rwkv_a2[i], rwkv_g1[i],
                           rwkv_g2[i], rwkv_k_k[i], rwkv_k_a[i], rwkv_r_k[i], rwkv_gn_g[i], rwkv_gn_b[i],
                           rwkv_w_o[i])
        else:
            j = l - n_a
            if j == 0:
                k_shared, v_shared = shared_kv(h, kv_norm_g, w_kv, k_norm_g)
            h = attn_layer(h, k_shared, v_shared, batch, seq, l, attn_norm_g[j], attn_w_q[j], q_norm_g[j], k_norm_g,
                           lambda_q1[j], lambda_k1[j], lambda_q2[j], lambda_k2[j], subln_g[j], attn_w_o[j])
        h = moe_layer(h, moe_norm_g[l], router_group_w[l], router_group_b[l], router_expert_w[l],
                      router_expert_b[l], expert_w_gate, expert_w_up, expert_w_down, l)
    return h.reshape(batch, seq, d)
```
